```python
import jax
import jax.numpy as jnp
from jax import lax
import numpy as np

D_MODEL = 2048
BATCH = 2
SEQ = 16384
DEPTH = 1
DEC_BATCH = 8
DEC_SEQ = 32
PAST_LEN = 2048

CHUNK = 64
GLA_HEADS = 4
GLA_DK = D_MODEL // (2 * GLA_HEADS)
GLA_DV = D_MODEL // GLA_HEADS
GLA_RANK = 16
GLA_TAU = 16.0
ATT_HEADS = 16
ATT_KV_HEADS = 4
HEAD_DIM = D_MODEL // ATT_HEADS
GROUP = ATT_HEADS // ATT_KV_HEADS
ROT_DIM = HEAD_DIM // 4
ROPE_THETA = 500000.0
IDX_HEADS = 8
IDX_DIM = 64
IDX_ROT = IDX_DIM // 4
TOPK_MAX = 256
Q_BLOCK = 128
N_EXPERTS = 32
TOP_K = 4
D_FF = D_MODEL
SWIGLU_LIMIT = 7.0
SWIGLU_ALPHA = 1.702
MOE_BLOCK = 128
EPS = 1e-6
IN_SPLITS = (GLA_HEADS * GLA_DK, GLA_HEADS * GLA_DK, GLA_HEADS * GLA_DV, GLA_HEADS * GLA_DV, GLA_RANK,
             ATT_HEADS * HEAD_DIM, ATT_KV_HEADS * HEAD_DIM, ATT_KV_HEADS * HEAD_DIM,
             IDX_HEADS * IDX_DIM, IDX_DIM, IDX_HEADS, D_MODEL, D_MODEL)
IN_TOTAL = sum(IN_SPLITS)

kernel_name = 'hybrid_gla_dsa_moe_stream_step'


def rmsnorm(x, g):
    xf = x.astype(jnp.float32)
    y = xf * lax.rsqrt(jnp.mean(xf * xf, axis=-1, keepdims=True) + EPS)
    return (y * g.astype(jnp.float32)).astype(x.dtype)


def partial_rope(x, pos, rot):
    half = rot // 2
    inv = ROPE_THETA ** (-2.0 * jnp.arange(half, dtype=jnp.float32) / rot)
    ang = pos.astype(jnp.float32)[:, None] * inv[None, :]
    cos = jnp.cos(ang)[:, None, :]
    sin = jnp.sin(ang)[:, None, :]
    xf = x.astype(jnp.float32)
    x1, x2 = xf[..., :half], xf[..., half:rot]
    out = jnp.concatenate([x1 * cos - x2 * sin, x2 * cos + x1 * sin, xf[..., rot:]], axis=-1)
    return out.astype(x.dtype)


def gla_chunk(S, xs):
    q, k, v, g = xs
    C = q.shape[2]
    b = jnp.cumsum(g, axis=2)
    qe = q * jnp.exp(b)
    ke = k * jnp.exp(-b)
    causal = jnp.tril(jnp.ones((C, C), dtype=bool))
    A = jnp.where(causal, jnp.einsum('bhtk,bhsk->bhts', qe, ke), 0.0)
    o = jnp.einsum('bhts,bhsv->bhtv', A, v) + jnp.einsum('bhtk,bhkv->bhtv', qe, S)
    b_end = b[:, :, -1:, :]
    S = jnp.exp(b_end[:, :, 0, :, None]) * S + jnp.einsum('bhsk,bhsv->bhkv', k * jnp.exp(b_end - b), v)
    return S, o


def gla_run(q, k, v, g, S0):
    B, T, H, _ = q.shape
    c = CHUNK if T % CHUNK == 0 else T
    n = T // c

    def chunks(t):
        return t.astype(jnp.float32).reshape(B, n, c, H, t.shape[-1]).transpose(1, 0, 3, 2, 4)

    S, o = lax.scan(gla_chunk, S0.astype(jnp.float32), (chunks(q), chunks(k), chunks(v), chunks(g)))
    o = o.transpose(1, 0, 3, 2, 4).reshape(B, T, H, v.shape[-1])
    return o, S


def sparse_block(q, qi, w, q_pos, K, V, Ki, topk):
    B, Q = q.shape[:2]
    L = K.shape[1]
    limit = (q_pos // CHUNK + 1) * CHUNK
    adm = jnp.arange(L, dtype=jnp.int32)[None, :] < limit[:, None]
    s = jnp.einsum('bqhd,bsd->bqhs', qi.astype(jnp.float32), Ki.astype(jnp.float32))
    score = jnp.einsum('bqhs,bqh->bqs', jax.nn.relu(s), w.astype(jnp.float32))
    score = jnp.where(adm[None], score, -jnp.inf)
    _, idx = lax.top_k(score, topk)
    valid = idx < limit[None, :, None]
    Ks = jax.vmap(lambda kb, ib: kb[ib])(K, idx)
    Vs = jax.vmap(lambda vb, ib: vb[ib])(V, idx)
    qg = q.reshape(B, Q, ATT_KV_HEADS, GROUP, HEAD_DIM)
    sc = jnp.einsum('bqhgd,bqshd->bqhgs', qg, Ks).astype(jnp.float32) * (HEAD_DIM ** -0.5)
    sc = jnp.where(valid[:, :, None, None, :], sc, -jnp.inf)
    p = jax.nn.softmax(sc, axis=-1)
    o = jnp.einsum('bqhgs,bqshd->bqhgd', p.astype(Vs.dtype), Vs)
    return o.reshape(B, Q, ATT_HEADS * HEAD_DIM)


def sparse_attention_prompt(q, qi, w, K, V, Ki):
    B, T = q.shape[:2]
    topk = min(TOPK_MAX, T // 4)
    nb = T // Q_BLOCK
    pos = jnp.arange(T, dtype=jnp.int32).reshape(nb, Q_BLOCK)

    def blocks(t):
        return t.reshape(B, nb, Q_BLOCK, *t.shape[2:]).swapaxes(0, 1)

    o = lax.map(lambda a: sparse_block(a[0], a[1], a[2], a[3], K, V, Ki, topk),
                (blocks(q), blocks(qi), blocks(w), pos))
    return o.swapaxes(0, 1).reshape(B, T, ATT_HEADS * HEAD_DIM)


def moe_ffn(x, w_router, b_router, w_gu, b_gu, w_down, b_down):
    N, D = x.shape
    logits = jnp.dot(x, w_router).astype(jnp.float32) + b_router.astype(jnp.float32)
    top_v, top_i = lax.top_k(logits, TOP_K)
    gates = jax.nn.softmax(top_v, axis=-1)
    flat_e = top_i.reshape(-1)
    order = jnp.argsort(flat_e)
    e_sorted = flat_e[order]
    tok_sorted = order // TOP_K
    g_sorted = gates.reshape(-1)[order]
    pad = (-(N * TOP_K)) % MOE_BLOCK
    e_sorted = jnp.pad(e_sorted, (0, pad), mode='edge')
    tok_sorted = jnp.pad(tok_sorted, (0, pad))
    g_sorted = jnp.pad(g_sorted, (0, pad))
    nb = (N * TOP_K + pad) // MOE_BLOCK
    xb = x[tok_sorted].reshape(nb, MOE_BLOCK, D)
    eb = e_sorted.reshape(nb, MOE_BLOCK)

    def block(args):
        xs, es = args

        def body(e, acc):
            gu = jnp.dot(xs, w_gu[e]).astype(jnp.float32) + b_gu[e].astype(jnp.float32)
            gate = jnp.minimum(gu[:, :D_FF], SWIGLU_LIMIT)
            up = jnp.clip(gu[:, D_FF:], -SWIGLU_LIMIT, SWIGLU_LIMIT)
            h = (up + 1.0) * gate * jax.nn.sigmoid(SWIGLU_ALPHA * gate)
            out = jnp.dot(h.astype(xs.dtype), w_down[e]).astype(jnp.float32) + b_down[e].astype(jnp.float32)
            return acc + jnp.where((es == e)[:, None], out, 0.0)

        return lax.fori_loop(es[0], es[-1] + 1, body, jnp.zeros((MOE_BLOCK, D), jnp.float32))

    out = lax.map(block, (xb, eb)).reshape(-1, D)
    y = jax.ops.segment_sum(out * g_sorted[:, None], tok_sorted, num_segments=N)
    return y.astype(x.dtype)


def setup_inputs(seed: int = 0) -> dict:
    key = jax.random.key(seed)
    ks = jax.random.split(key, 24)

    def nrm(k, shape, scale):
        return jax.random.normal(k, shape, jnp.float32) * scale

    L = DEPTH
    return {
        'x_prompt': nrm(ks[0], (BATCH, SEQ, D_MODEL), 1.0),
        'x_sample': nrm(ks[1], (DEC_BATCH, DEC_SEQ, D_MODEL), 1.0),
        'cache_k': nrm(ks[2], (L, DEC_BATCH, PAST_LEN, ATT_KV_HEADS, HEAD_DIM), 1.0),
        'cache_v': nrm(ks[3], (L, DEC_BATCH, PAST_LEN, ATT_KV_HEADS, HEAD_DIM), 1.0),
        'cache_kidx': nrm(ks[4], (L, DEC_BATCH, PAST_LEN, IDX_DIM), 1.0),
        'state_gla': nrm(ks[5], (L, DEC_BATCH, GLA_HEADS, GLA_DK, GLA_DV), 1.0),
        'norm1_g': 1.0 + nrm(ks[6], (L, D_MODEL), 0.02),
        'w_in': nrm(ks[7], (L, D_MODEL, IN_TOTAL), D_MODEL ** -0.5),
        'w_alpha2': nrm(ks[8], (L, GLA_RANK, GLA_HEADS * GLA_DK), GLA_RANK ** -0.5),
        'b_alpha': nrm(ks[9], (L, GLA_HEADS * GLA_DK), 0.1),
        'gla_norm_g': 1.0 + nrm(ks[10], (L, GLA_DV), 0.02),
        'q_norm_g': 1.0 + nrm(ks[11], (L, HEAD_DIM), 0.02),
        'k_norm_g': 1.0 + nrm(ks[12], (L, HEAD_DIM), 0.02),
        'w_proj_a': nrm(ks[13], (L, GLA_HEADS * GLA_DV, D_MODEL), (GLA_HEADS * GLA_DV) ** -0.5),
        'w_proj_b': nrm(ks[14], (L, ATT_HEADS * HEAD_DIM, D_MODEL), (ATT_HEADS * HEAD_DIM) ** -0.5),
        'w_out': nrm(ks[15], (L, D_MODEL, D_MODEL), D_MODEL ** -0.5),
        'norm2_g': 1.0 + nrm(ks[16], (L, D_MODEL), 0.02),
        'w_router': nrm(ks[17], (L, D_MODEL, N_EXPERTS), D_MODEL ** -0.5),
        'b_router': nrm(ks[18], (L, N_EXPERTS), 0.01),
        'w_gu': nrm(ks[19], (L, N_EXPERTS, D_MODEL, 2 * D_FF), D_MODEL ** -0.5),
        'b_gu': nrm(ks[20], (L, N_EXPERTS, 2 * D_FF), 0.01),
        'w_down': nrm(ks[21], (L, N_EXPERTS, D_FF, D_MODEL), D_FF ** -0.5),
        'b_down': nrm(ks[22], (L, N_EXPERTS, D_MODEL), 0.01),
    }


def reference(x_prompt, x_sample, cache_k, cache_v, cache_kidx, state_gla, norm1_g, w_in, w_alpha2, b_alpha,
              gla_norm_g, q_norm_g, k_norm_g, w_proj_a, w_proj_b, w_out, norm2_g, w_router, b_router,
              w_gu, b_gu, w_down, b_down):
    B, T, D = x_prompt.shape
    DB, DS, _ = x_sample.shape
    NP = B * T
    past = cache_k.shape[2]
    pos_p = jnp.arange(T, dtype=jnp.int32)
    pos_s = past + jnp.arange(DS, dtype=jnp.int32)
    cols = [int(c) for c in np.cumsum(IN_SPLITS)[:-1]]

    def groups(t, *tail):
        return t[:NP].reshape(B, T, *tail), t[NP:].reshape(DB, DS, *tail)

    x = jnp.concatenate([x_prompt.reshape(NP, D), x_sample.reshape(DB * DS, D)], axis=0)
    ks_p, vs_p, kis_p, ss_p = [], [], [], []
    ks_s, vs_s, kis_s, ss_s = [], [], [], []
    for l in range(DEPTH):
        xn = rmsnorm(x, norm1_g[l])
        z = jnp.dot(xn, w_in[l])
        gq, gk, gv, gr, ga, aq, ak, av, iq, ik, iw, za, zb = jnp.split(z, cols, axis=-1)

        g_log = jax.nn.log_sigmoid((jnp.dot(ga, w_alpha2[l]) + b_alpha[l]).astype(jnp.float32)) / GLA_TAU
        gq_p, gq_s = groups(gq * (GLA_DK ** -0.5), GLA_HEADS, GLA_DK)
        gk_p, gk_s = groups(gk, GLA_HEADS, GLA_DK)
        gv_p, gv_s = groups(gv, GLA_HEADS, GLA_DV)
        gg_p, gg_s = groups(g_log, GLA_HEADS, GLA_DK)
        o_a_p, S_p = gla_run(gq_p, gk_p, gv_p, gg_p, jnp.zeros((B, GLA_HEADS, GLA_DK, GLA_DV), jnp.float32))
        o_a_s, S_s = gla_run(gq_s, gk_s, gv_s, gg_s, state_gla[l])
        o_a = jnp.concatenate([o_a_p.reshape(NP, GLA_HEADS, GLA_DV),
                               o_a_s.reshape(DB * DS, GLA_HEADS, GLA_DV)], axis=0)
        o_a = rmsnorm(o_a, gla_norm_g[l]).astype(x.dtype) * jax.nn.silu(gr.reshape(-1, GLA_HEADS, GLA_DV))
        o_a = o_a.reshape(-1, GLA_HEADS * GLA_DV)

        qa = rmsnorm(aq.reshape(-1, ATT_HEADS, HEAD_DIM), q_norm_g[l])
        ka = rmsnorm(ak.reshape(-1, ATT_KV_HEADS, HEAD_DIM), k_norm_g[l])
        qa_p, qa_s = groups(qa, ATT_HEADS, HEAD_DIM)
        ka_p, ka_s = groups(ka, ATT_KV_HEADS, HEAD_DIM)
        va_p, va_s = groups(av, ATT_KV_HEADS, HEAD_DIM)
        qi_p, qi_s = groups(iq, IDX_HEADS, IDX_DIM)
        ki_p, ki_s = groups(ik, 1, IDX_DIM)
        wi_p, wi_s = groups(iw * ((IDX_HEADS * IDX_DIM) ** -0.5), IDX_HEADS)
        qa_p = partial_rope(qa_p, pos_p, ROT_DIM)
        ka_p = partial_rope(ka_p, pos_p, ROT_DIM)
        qa_s = partial_rope(qa_s, pos_s, ROT_DIM)
        ka_s = partial_rope(ka_s, pos_s, ROT_DIM)
        qi_p = partial_rope(qi_p, pos_p, IDX_ROT)
        qi_s = partial_rope(qi_s, pos_s, IDX_ROT)
        ki_p = partial_rope(ki_p, pos_p, IDX_ROT)[:, :, 0]
        ki_s = partial_rope(ki_s, pos_s, IDX_ROT)[:, :, 0]
        o_b_p = sparse_attention_prompt(qa_p, qi_p, wi_p, ka_p, va_p, ki_p)
        k_all = jnp.concatenate([cache_k[l], ka_s], axis=1)
        v_all = jnp.concatenate([cache_v[l], va_s], axis=1)
        ki_all = jnp.concatenate([cache_kidx[l], ki_s], axis=1)
        o_b_s = sparse_block(qa_s, qi_s, wi_s, pos_s, k_all, v_all, ki_all, min(TOPK_MAX, k_all.shape[1] // 4))
        o_b = jnp.concatenate([o_b_p.reshape(NP, -1), o_b_s.reshape(DB * DS, -1)], axis=0)

        mixed = jax.nn.sigmoid(za) * jnp.dot(o_a, w_proj_a[l]) + jax.nn.sigmoid(zb) * jnp.dot(o_b, w_proj_b[l])
        x = x + jnp.dot(mixed, w_out[l])
        x = x + moe_ffn(rmsnorm(x, norm2_g[l]), w_router[l], b_router[l], w_gu[l], b_gu[l], w_down[l], b_down[l])

        ks_p.append(ka_p)
        vs_p.append(va_p)
        kis_p.append(ki_p)
        ss_p.append(S_p.astype(x_prompt.dtype))
        ks_s.append(ka_s)
        vs_s.append(va_s)
        kis_s.append(ki_s)
        ss_s.append(S_s.astype(state_gla.dtype))

    y_prompt = x[:NP].reshape(B, T, D)
    y_sample = x[NP:].reshape(DB, DS, D)
    k_prompt = jnp.stack(ks_p, axis=0)
    v_prompt = jnp.stack(vs_p, axis=0)
    kidx_prompt = jnp.stack(kis_p, axis=0)
    gla_prompt = jnp.stack(ss_p, axis=0)
    k_sample = jnp.stack(ks_s, axis=0)
    v_sample = jnp.stack(vs_s, axis=0)
    kidx_sample = jnp.stack(kis_s, axis=0)
    gla_sample = jnp.stack(ss_s, axis=0)
    return (y_prompt, y_sample, k_prompt, v_prompt, kidx_prompt, gla_prompt, k_sample, v_sample, kidx_sample, gla_sample)
```

```python
import functools
import math

import jax
import jax.numpy as jnp
from jax import lax
from jax.experimental import pallas as pl
from jax.experimental.pallas import tpu as pltpu

CHUNK = 64
GLA_HEADS = 4
GLA_RANK = 16
GLA_TAU = 16.0
ATT_HEADS = 16
ATT_KV_HEADS = 4
GROUP = ATT_HEADS // ATT_KV_HEADS
ROPE_THETA = 500000.0
IDX_HEADS = 8
IDX_DIM = 64
TOPK_MAX = 256
N_EXPERTS = 32
TOP_K = 4
SWIGLU_LIMIT = 7.0
SWIGLU_ALPHA = 1.702
EPS = 1e-6

LANES = 128
VMEM_LIMIT = 56 * 1024 * 1024
INT_MIN = -(2 ** 31)
NEG_INIT = -1e30

_HIGHEST = lax.Precision.HIGHEST
_NT = (((1,), (1,)), ((), ()))
_TN = (((0,), (0,)), ((), ()))


def _pick(n, cands):
    for c in cands:
        if n % c == 0:
            return c
    raise ValueError(f"no tile in {cands} divides {n}")


def _cparams(sem):
    return pltpu.CompilerParams(dimension_semantics=sem, vmem_limit_bytes=VMEM_LIMIT)


def _rmsnorm_kernel(x_ref, g_ref, o_ref):
    x = x_ref[...]
    y = x * lax.rsqrt(jnp.mean(x * x, axis=-1, keepdims=True) + EPS) * g_ref[...]
    o_ref[...] = y.astype(o_ref.dtype)


def _rmsnorm(x, g, out_dtype):
    n, d = x.shape
    tm = _pick(n, (768, 512, 256, 128, 64))
    return pl.pallas_call(
        _rmsnorm_kernel,
        grid=(n // tm,),
        in_specs=[pl.BlockSpec((tm, d), lambda i: (i, 0)), pl.BlockSpec((1, d), lambda i: (0, 0))],
        out_specs=pl.BlockSpec((tm, d), lambda i: (i, 0)),
        out_shape=jax.ShapeDtypeStruct((n, d), out_dtype),
        compiler_params=_cparams(("parallel",)),
        name="rmsnorm",
    )(x, g.reshape(1, d))


def _mm_kernel(a_ref, w_ref, o_ref):
    o_ref[...] = jnp.dot(a_ref[...], w_ref[...], preferred_element_type=jnp.float32).astype(o_ref.dtype)


def _matmul(a, w, out_dtype=jnp.float32, name="matmul"):
    m, k = a.shape
    _, n = w.shape
    tm = _pick(m, (768, 512, 256, 128, 64))
    tn = 1024 if n % 1024 == 0 else (512 if n % 512 == 0 else n)
    return pl.pallas_call(
        _mm_kernel,
        grid=(m // tm, n // tn),
        in_specs=[pl.BlockSpec((tm, k), lambda i, j: (i, 0)), pl.BlockSpec((k, tn), lambda i, j: (0, j))],
        out_specs=pl.BlockSpec((tm, tn), lambda i, j: (i, j)),
        out_shape=jax.ShapeDtypeStruct((m, n), out_dtype),
        compiler_params=_cparams(("parallel", "arbitrary")),
        name=name,
    )(a, w)


def _merge_kernel(oa_ref, ob_ref, wa_ref, wb_ref, za_ref, zb_ref, o_ref):
    pa = jnp.dot(oa_ref[...], wa_ref[...], preferred_element_type=jnp.float32)
    pb = jnp.dot(ob_ref[...], wb_ref[...], preferred_element_type=jnp.float32)
    o_ref[...] = (jax.nn.sigmoid(za_ref[...]) * pa + jax.nn.sigmoid(zb_ref[...]) * pb).astype(o_ref.dtype)


def _merge(o_a, o_b, wa, wb, z4):
    m, k = o_a.shape
    n = wa.shape[1]
    tm = _pick(m, (768, 512, 256, 128, 64))
    tn = 512
    nj = n // tn
    return pl.pallas_call(
        _merge_kernel,
        grid=(m // tm, nj),
        in_specs=[
            pl.BlockSpec((tm, k), lambda i, j: (i, 0)),
            pl.BlockSpec((tm, k), lambda i, j: (i, 0)),
            pl.BlockSpec((k, tn), lambda i, j: (0, j)),
            pl.BlockSpec((k, tn), lambda i, j: (0, j)),
            pl.BlockSpec((tm, tn), lambda i, j: (i, j)),
            pl.BlockSpec((tm, tn), lambda i, j: (i, j + nj)),
        ],
        out_specs=pl.BlockSpec((tm, tn), lambda i, j: (i, j)),
        out_shape=jax.ShapeDtypeStruct((m, n), jnp.bfloat16),
        compiler_params=_cparams(("parallel", "arbitrary")),
        name="merge",
    )(o_a, o_b, wa, wb, z4, z4)


def _resid_mm_kernel(x_ref, a_ref, w_ref, o_ref):
    o_ref[...] = x_ref[...] + jnp.dot(a_ref[...], w_ref[...], preferred_element_type=jnp.float32)


def _resid_matmul(x, a, w):
    m, k = a.shape
    n = w.shape[1]
    tm = _pick(m, (768, 512, 256, 128, 64))
    tn = 1024 if n % 1024 == 0 else n
    return pl.pallas_call(
        _resid_mm_kernel,
        grid=(m // tm, n // tn),
        in_specs=[
            pl.BlockSpec((tm, tn), lambda i, j: (i, j)),
            pl.BlockSpec((tm, k), lambda i, j: (i, 0)),
            pl.BlockSpec((k, tn), lambda i, j: (0, j)),
        ],
        out_specs=pl.BlockSpec((tm, tn), lambda i, j: (i, j)),
        out_shape=jax.ShapeDtypeStruct((m, n), jnp.float32),
        compiler_params=_cparams(("parallel", "arbitrary")),
        name="out_proj",
    )(x, a, w)


def _log_sigmoid(x):
    return jnp.minimum(x, 0.0) - jnp.log1p(jnp.exp(-jnp.abs(x)))


def _gla_kernel(*refs, c, dk, dv, has_init):
    if has_init:
        (z1_ref, ga_ref, w2_ref, w2t_ref, b_ref, bcol_ref, gn_ref, s0_ref, o_ref, sout_ref, s_scr) = refs
    else:
        (z1_ref, ga_ref, w2_ref, w2t_ref, b_ref, bcol_ref, gn_ref, o_ref, sout_ref, s_scr) = refs
        s0_ref = None
    ci = pl.program_id(1)
    hk = GLA_HEADS * dk

    @pl.when(ci == 0)
    def _():
        if has_init:
            s_scr[...] = s0_ref[0]
        else:
            s_scr[...] = jnp.zeros_like(s_scr)

    ga = ga_ref[...]
    g = _log_sigmoid(jnp.dot(ga, w2_ref[...], precision=_HIGHEST,
                             preferred_element_type=jnp.float32) + b_ref[...]) / GLA_TAU
    row = lax.broadcasted_iota(jnp.int32, (c, c), 0)
    col = lax.broadcasted_iota(jnp.int32, (c, c), 1)
    causal = col <= row
    b = jnp.dot(causal.astype(jnp.float32), g, precision=_HIGHEST, preferred_element_type=jnp.float32)
    gt = _log_sigmoid(lax.dot_general(w2t_ref[...], ga, _NT, precision=_HIGHEST,
                                      preferred_element_type=jnp.float32) + bcol_ref[...]) / GLA_TAU
    bend_col = jnp.sum(gt, axis=1, keepdims=True)
    scale = dk ** -0.5
    for h in range(GLA_HEADS):
        bh = b[:, h * dk:(h + 1) * dk]
        q = z1_ref[:, h * dk:(h + 1) * dk] * scale
        k = z1_ref[:, hk + h * dk: hk + (h + 1) * dk]
        v = z1_ref[:, 2 * hk + h * dv: 2 * hk + (h + 1) * dv].astype(jnp.bfloat16)
        r = z1_ref[:, 2 * hk + GLA_HEADS * dv + h * dv: 2 * hk + GLA_HEADS * dv + (h + 1) * dv]
        qe = (q * jnp.exp(bh)).astype(jnp.bfloat16)
        ke = (k * jnp.exp(-bh)).astype(jnp.bfloat16)
        a = lax.dot_general(qe, ke, _NT, preferred_element_type=jnp.float32)
        a = jnp.where(causal, a, 0.0).astype(jnp.bfloat16)
        s = s_scr[h]
        o = (jnp.dot(a, v, preferred_element_type=jnp.float32)
             + jnp.dot(qe, s.astype(jnp.bfloat16), preferred_element_type=jnp.float32))
        b_end = bh[c - 1:c, :]
        kd = (k * jnp.exp(b_end - bh)).astype(jnp.bfloat16)
        s_scr[h] = (jnp.exp(bend_col[h * dk:(h + 1) * dk, :]) * s
                    + lax.dot_general(kd, v, _TN, preferred_element_type=jnp.float32))
        on = o * lax.rsqrt(jnp.mean(o * o, axis=-1, keepdims=True) + EPS) * gn_ref[...]
        o_ref[:, h * dv:(h + 1) * dv] = (on * (r * jax.nn.sigmoid(r))).astype(o_ref.dtype)

    @pl.when(ci == pl.num_programs(1) - 1)
    def _():
        sout_ref[0] = s_scr[...]


def _gla(z1, z3, row0, n_seq, seq_len, w2p, w2t, b_alpha, gn, s0, dk, dv):
    c = CHUNK if seq_len % CHUNK == 0 else seq_len
    nc = seq_len // c
    hk = GLA_HEADS * dk
    blk0 = row0 // c
    ga_blk = (IDX_HEADS + 1)
    has_init = s0 is not None
    rows = n_seq * seq_len
    in_specs = [
        pl.BlockSpec((c, z1.shape[1]), lambda s, i: (blk0 + s * nc + i, 0)),
        pl.BlockSpec((c, LANES), lambda s, i: (blk0 + s * nc + i, ga_blk)),
        pl.BlockSpec((LANES, hk), lambda s, i: (0, 0)),
        pl.BlockSpec((hk, LANES), lambda s, i: (0, 0)),
        pl.BlockSpec((1, hk), lambda s, i: (0, 0)),
        pl.BlockSpec((hk, 1), lambda s, i: (0, 0)),
        pl.BlockSpec((1, dv), lambda s, i: (0, 0)),
    ]
    args = [z1, z3, w2p, w2t, b_alpha.reshape(1, hk), b_alpha.reshape(hk, 1), gn.reshape(1, dv)]
    if has_init:
        in_specs.append(pl.BlockSpec((1, GLA_HEADS, dk, dv), lambda s, i: (s, 0, 0, 0)))
        args.append(s0)
    return pl.pallas_call(
        functools.partial(_gla_kernel, c=c, dk=dk, dv=dv, has_init=has_init),
        grid=(n_seq, nc),
        in_specs=in_specs,
        out_specs=[
            pl.BlockSpec((c, GLA_HEADS * dv), lambda s, i: (s * nc + i, 0)),
            pl.BlockSpec((1, GLA_HEADS, dk, dv), lambda s, i: (s, 0, 0, 0)),
        ],
        out_shape=[
            jax.ShapeDtypeStruct((rows, GLA_HEADS * dv), jnp.bfloat16),
            jax.ShapeDtypeStruct((n_seq, GLA_HEADS, dk, dv), jnp.float32),
        ],
        scratch_shapes=[pltpu.VMEM((GLA_HEADS, dk, dv), jnp.float32)],
        compiler_params=_cparams(("parallel", "arbitrary")),
        name="gla_init" if has_init else "gla",
    )(*args)


def _rope(x, cos, sin, half):
    lane = lax.broadcasted_iota(jnp.int32, x.shape, 1)
    partner = jnp.where(lane < half, pltpu.roll(x, LANES - half, 1), pltpu.roll(x, half, 1))
    return x * cos + partner * sin


def _split_hi_lo(x):
    hi = x.astype(jnp.bfloat16).astype(jnp.float32)
    lo = (x - hi).astype(jnp.bfloat16).astype(jnp.float32)
    return hi, lo


def _prep_kernel(z2_ref, z3_ref, qg_ref, kg_ref, ca_ref, sa_ref, ci_ref, si_ref,
                 q_ref, kf_ref, kb_ref, vb_ref, qc_ref, kif_ref, kc_ref, w_ref, *, hd):
    ca, sa, ci, si = ca_ref[...], sa_ref[...], ci_ref[...], si_ref[...]
    qscale = hd ** -0.5

    def headnorm(x, g):
        return x * lax.rsqrt(jnp.mean(x * x, axis=-1, keepdims=True) + EPS) * g

    for h in range(ATT_HEADS):
        x = headnorm(z2_ref[:, h * hd:(h + 1) * hd], qg_ref[...])
        q_ref[h] = (_rope(x, ca, sa, hd // 8) * qscale).astype(q_ref.dtype)
    k0 = ATT_HEADS * hd
    for h in range(ATT_KV_HEADS):
        x = headnorm(z2_ref[:, k0 + h * hd:k0 + (h + 1) * hd], kg_ref[...])
        y = _rope(x, ca, sa, hd // 8)
        kf_ref[:, h * hd:(h + 1) * hd] = y
        kb_ref[:, h * hd:(h + 1) * hd] = y.astype(kb_ref.dtype)
    v0 = k0 + ATT_KV_HEADS * hd
    vb_ref[...] = z2_ref[:, v0:v0 + ATT_KV_HEADS * hd].astype(vb_ref.dtype)
    for h in range(IDX_HEADS):
        y = _rope(z3_ref[:, h * LANES:(h + 1) * LANES], ci, si, IDX_DIM // 8)
        hi, lo = _split_hi_lo(y)
        t = (hi + pltpu.roll(lo, IDX_DIM, 1)).astype(qc_ref.dtype)
        qc_ref[h] = jnp.concatenate([t, t], axis=1)
    y = _rope(z3_ref[:, IDX_HEADS * LANES:(IDX_HEADS + 1) * LANES], ci, si, IDX_DIM // 8)
    kif_ref[...] = y[:, :IDX_DIM]
    hi, lo = _split_hi_lo(y)
    kc_ref[...] = jnp.concatenate([hi + pltpu.roll(hi, IDX_DIM, 1), lo + pltpu.roll(lo, IDX_DIM, 1)],
                                  axis=1).astype(kc_ref.dtype)
    w_ref[...] = z3_ref[:, (IDX_HEADS + 2) * LANES:(IDX_HEADS + 3) * LANES] * ((IDX_HEADS * IDX_DIM) ** -0.5)


def _rope_tables(pos, rot):
    half = rot // 2
    inv = ROPE_THETA ** (-2.0 * jnp.arange(half, dtype=jnp.float32) / rot)
    ang = pos.astype(jnp.float32)[:, None] * inv[None, :]
    cos, sin = jnp.cos(ang), jnp.sin(ang)
    n = pos.shape[0]
    ones = jnp.ones((n, LANES - rot), jnp.float32)
    zeros = jnp.zeros((n, LANES - rot), jnp.float32)
    return (jnp.concatenate([cos, cos, ones], axis=1), jnp.concatenate([-sin, sin, zeros], axis=1))


def _prep(z2, z3, q_norm_g, k_norm_g, pos, hd):
    n = z2.shape[0]
    tr = _pick(n, (256, 128, 64))
    ca, sa = _rope_tables(pos, hd // 4)
    ci, si = _rope_tables(pos, IDX_DIM // 4)
    row = lambda i: (i, 0)
    tab = pl.BlockSpec((tr, LANES), row)
    kvw = ATT_KV_HEADS * hd
    return pl.pallas_call(
        functools.partial(_prep_kernel, hd=hd),
        grid=(n // tr,),
        in_specs=[
            pl.BlockSpec((tr, z2.shape[1]), row),
            pl.BlockSpec((tr, z3.shape[1]), row),
            pl.BlockSpec((1, hd), lambda i: (0, 0)),
            pl.BlockSpec((1, hd), lambda i: (0, 0)),
            tab, tab, tab, tab,
        ],
        out_specs=[
            pl.BlockSpec((ATT_HEADS, tr, hd), lambda i: (0, i, 0)),
            pl.BlockSpec((tr, kvw), row),
            pl.BlockSpec((tr, kvw), row),
            pl.BlockSpec((tr, kvw), row),
            pl.BlockSpec((IDX_HEADS, tr, 4 * IDX_DIM), lambda i: (0, i, 0)),
            pl.BlockSpec((tr, IDX_DIM), row),
            pl.BlockSpec((tr, 4 * IDX_DIM), row),
            pl.BlockSpec((tr, LANES), row),
        ],
        out_shape=[
            jax.ShapeDtypeStruct((ATT_HEADS, n, hd), jnp.bfloat16),
            jax.ShapeDtypeStruct((n, kvw), jnp.float32),
            jax.ShapeDtypeStruct((n, kvw), jnp.bfloat16),
            jax.ShapeDtypeStruct((n, kvw), jnp.bfloat16),
            jax.ShapeDtypeStruct((IDX_HEADS, n, 4 * IDX_DIM), jnp.bfloat16),
            jax.ShapeDtypeStruct((n, IDX_DIM), jnp.float32),
            jax.ShapeDtypeStruct((n, 4 * IDX_DIM), jnp.bfloat16),
            jax.ShapeDtypeStruct((n, LANES), jnp.float32),
        ],
        compiler_params=_cparams(("parallel",)),
        name="attn_prep",
    )(z2, z3, q_norm_g.reshape(1, hd), k_norm_g.reshape(1, hd), ca, sa, ci, si)


def _needed_tiles(i, tq, tk, lk_true, pos_off):
    last = i * tq + pos_off + tq - 1
    max_limit = jnp.minimum((last // CHUNK + 1) * CHUNK, lk_true)
    return (max_limit + tk - 1) // tk


def _attn_kernel(q_ref, qc_ref, w_ref, kc_ref, k_ref, v_ref, o_ref,
                 keys_ref, thr_ref, m_ref, l_ref, acc_ref, *, tq, tk, n_kt, lk_true, pos_off, topk, nbits, hd):
    i = pl.program_id(1)
    kt = pl.program_id(2)
    n_need = _needed_tiles(i, tq, tk, lk_true, pos_off)
    n_sub = tk // LANES

    def tile(j):
        return pl.ds(pl.multiple_of(j * tk, tk), tk)

    @pl.when(kt == 0)
    def _select():
        rowi = lax.broadcasted_iota(jnp.int32, (tq, 1), 0)
        limit = jnp.minimum(((i * tq + pos_off + rowi) // CHUNK + 1) * CHUNK, lk_true)
        k_eff = jnp.minimum(topk, limit).astype(jnp.float32)
        w = w_ref[...]

        def score_body(j, carry):
            kc = kc_ref[0, tile(j), :]
            acc = jnp.zeros((tq, tk), jnp.float32)
            for h in range(IDX_HEADS):
                s = lax.dot_general(qc_ref[h], kc, _NT, preferred_element_type=jnp.float32)
                acc = acc + jnp.maximum(s, 0.0) * w[:, h:h + 1]
            bits = pltpu.bitcast(acc, jnp.int32)
            sgn = bits >> 31
            key = (bits ^ (sgn & 0x7FFFFFFF)) - sgn
            col = j * tk + lax.broadcasted_iota(jnp.int32, (tq, tk), 1)
            keys_ref[:, tile(j)] = jnp.where(col < limit, key, INT_MIN)
            return carry

        lax.fori_loop(0, n_need, score_body, 0)

        def count(pred):
            def body(j, acc):
                kk = keys_ref[:, tile(j)]
                col = j * tk + lax.broadcasted_iota(jnp.int32, (tq, tk), 1)
                f = jnp.where(pred(kk, col), 1.0, 0.0)
                for c in range(n_sub):
                    acc = acc + f[:, c * LANES:(c + 1) * LANES]
                return acc
            acc = lax.fori_loop(0, n_need, body, jnp.zeros((tq, LANES), jnp.float32))
            return jnp.sum(acc, axis=1, keepdims=True)

        def bit_body(b, cand):
            t = cand + lax.shift_left(jnp.int32(1), 31 - b)
            c = count(lambda kk, col: kk >= t)
            return jnp.where(c >= k_eff, t, cand)

        thr = lax.fori_loop(0, 32, bit_body, jnp.full((tq, 1), INT_MIN, jnp.int32))
        thr_ref[...] = thr
        c_gt = count(lambda kk, col: kk > thr)
        c_ge = count(lambda kk, col: kk >= thr)
        need = k_eff - c_gt
        excess = jnp.max(c_ge - k_eff) > 0.5

        @pl.when(excess)
        def _ties():
            def idx_body(b, cut):
                t = cut + lax.shift_left(jnp.int32(1), nbits - 1 - b)
                c = count(lambda kk, col: jnp.where(kk == thr, col, t) < t)
                return jnp.where(c <= need, t, cut)

            cut = lax.fori_loop(0, nbits, idx_body, jnp.zeros((tq, 1), jnp.int32))

            def fix_body(j, carry):
                kk = keys_ref[:, tile(j)]
                col = j * tk + lax.broadcasted_iota(jnp.int32, (tq, tk), 1)
                drop = jnp.where(kk == thr, col, -1) >= cut
                keys_ref[:, tile(j)] = jnp.where(drop, INT_MIN, kk)
                return carry

            lax.fori_loop(0, n_need, fix_body, 0)

        m_ref[...] = jnp.full(m_ref.shape, NEG_INIT, jnp.float32)
        l_ref[...] = jnp.zeros(l_ref.shape, jnp.float32)
        acc_ref[...] = jnp.zeros(acc_ref.shape, jnp.float32)

    @pl.when(kt < n_need)
    def _attend():
        sel = keys_ref[:, tile(kt)] >= thr_ref[...]
        for kvh in range(ATT_KV_HEADS):
            q4 = q_ref[kvh * GROUP:(kvh + 1) * GROUP].reshape(GROUP * tq, hd)
            kb = k_ref[0, :, kvh * hd:(kvh + 1) * hd]
            vb = v_ref[0, :, kvh * hd:(kvh + 1) * hd]
            s = lax.dot_general(q4, kb, _NT, preferred_element_type=jnp.float32)
            s = jnp.concatenate([jnp.where(sel, s[g * tq:(g + 1) * tq], -jnp.inf) for g in range(GROUP)], axis=0)
            m_prev = m_ref[kvh]
            m_new = jnp.maximum(m_prev, jnp.max(s, axis=1, keepdims=True))
            p = jnp.exp(s - m_new)
            alpha = jnp.exp(m_prev - m_new)
            l_ref[kvh] = alpha * l_ref[kvh] + jnp.sum(p, axis=1, keepdims=True)
            acc_ref[kvh] = alpha * acc_ref[kvh] + jnp.dot(p.astype(vb.dtype), vb, preferred_element_type=jnp.float32)
            m_ref[kvh] = m_new

    @pl.when(kt == n_kt - 1)
    def _finish():
        for kvh in range(ATT_KV_HEADS):
            o = acc_ref[kvh] / l_ref[kvh]
            for g in range(GROUP):
                h = kvh * GROUP + g
                o_ref[:, h * hd:(h + 1) * hd] = o[g * tq:(g + 1) * tq].astype(o_ref.dtype)


def _sparse_attention(q_hm, qcat_hm, w_pad, kcat, kb, vb, n_grp, tq_total, lk_true, pos_off, topk, tq, tk):
    hd = q_hm.shape[-1]
    lk_pad = kb.shape[1]
    nq = tq_total // tq
    n_kt = lk_pad // tk
    nbits = max(1, math.ceil(math.log2(lk_pad + 1)))
    need = functools.partial(_needed_tiles, tq=tq, tk=tk, lk_true=lk_true, pos_off=pos_off)
    kv_map = lambda g, i, kt: (g, jnp.minimum(kt, need(i) - 1), 0)
    qrow = lambda g, i, kt: (g * nq + i, 0)
    kern = functools.partial(_attn_kernel, tq=tq, tk=tk, n_kt=n_kt, lk_true=lk_true, pos_off=pos_off,
                             topk=topk, nbits=nbits, hd=hd)
    return pl.pallas_call(
        kern,
        grid=(n_grp, nq, n_kt),
        in_specs=[
            pl.BlockSpec((ATT_HEADS, tq, hd), lambda g, i, kt: (0, g * nq + i, 0)),
            pl.BlockSpec((IDX_HEADS, tq, 4 * IDX_DIM), lambda g, i, kt: (0, g * nq + i, 0)),
            pl.BlockSpec((tq, LANES), qrow),
            pl.BlockSpec((1, lk_pad, 4 * IDX_DIM), lambda g, i, kt: (g, 0, 0)),
            pl.BlockSpec((1, tk, ATT_KV_HEADS * hd), kv_map),
            pl.BlockSpec((1, tk, ATT_KV_HEADS * hd), kv_map),
        ],
        out_specs=pl.BlockSpec((tq, ATT_HEADS * hd), qrow),
        out_shape=jax.ShapeDtypeStruct((n_grp * tq_total, ATT_HEADS * hd), jnp.bfloat16),
        scratch_shapes=[
            pltpu.VMEM((tq, lk_pad), jnp.int32),
            pltpu.VMEM((tq, 1), jnp.int32),
            pltpu.VMEM((ATT_KV_HEADS, GROUP * tq, 1), jnp.float32),
            pltpu.VMEM((ATT_KV_HEADS, GROUP * tq, 1), jnp.float32),
            pltpu.VMEM((ATT_KV_HEADS, GROUP * tq, hd), jnp.float32),
        ],
        compiler_params=_cparams(("parallel", "arbitrary", "arbitrary")),
        name="sparse_attn",
    )(q_hm, qcat_hm, w_pad, kcat, kb, vb)


def _router_kernel(h_ref, g_ref, wr_ref, br_ref, xn_ref, ti_ref, tg_ref):
    h = h_ref[...]
    xn = h * lax.rsqrt(jnp.mean(h * h, axis=-1, keepdims=True) + EPS) * g_ref[...]
    xn_ref[...] = xn
    logits = jnp.dot(xn, wr_ref[...], precision=_HIGHEST, preferred_element_type=jnp.float32) + br_ref[...]
    lane = lax.broadcasted_iota(jnp.int32, logits.shape, 1)
    cur = logits
    vals, idxs = [], []
    for _ in range(TOP_K):
        m = jnp.max(cur, axis=1, keepdims=True)
        idx = jnp.min(jnp.where(cur == m, lane, LANES), axis=1, keepdims=True)
        vals.append(m)
        idxs.append(idx)
        cur = jnp.where(lane == idx, -jnp.inf, cur)
    es = [jnp.exp(v - vals[0]) for v in vals]
    denom = es[0] + es[1] + es[2] + es[3]
    ti = jnp.zeros(logits.shape, jnp.int32)
    tg = jnp.zeros(logits.shape, jnp.float32)
    for k in range(TOP_K):
        ti = jnp.where(lane == k, idxs[k], ti)
        tg = jnp.where(lane == k, es[k] / denom, tg)
    ti_ref[...] = ti
    tg_ref[...] = tg


def _router(h, g, w_router, b_router):
    n, d = h.shape
    tm = _pick(n, (768, 512, 256, 128, 64))
    wr = jnp.pad(w_router, ((0, 0), (0, LANES - N_EXPERTS)))
    br = jnp.pad(b_router, (0, LANES - N_EXPERTS), constant_values=NEG_INIT).reshape(1, LANES)
    row = lambda i: (i, 0)
    return pl.pallas_call(
        _router_kernel,
        grid=(n // tm,),
        in_specs=[
            pl.BlockSpec((tm, d), row),
            pl.BlockSpec((1, d), lambda i: (0, 0)),
            pl.BlockSpec((d, LANES), lambda i: (0, 0)),
            pl.BlockSpec((1, LANES), lambda i: (0, 0)),
        ],
        out_specs=[pl.BlockSpec((tm, d), row), pl.BlockSpec((tm, LANES), row), pl.BlockSpec((tm, LANES), row)],
        out_shape=[
            jax.ShapeDtypeStruct((n, d), jnp.float32),
            jax.ShapeDtypeStruct((n, LANES), jnp.int32),
            jax.ShapeDtypeStruct((n, LANES), jnp.float32),
        ],
        compiler_params=_cparams(("parallel",)),
        name="router",
    )(h, g.reshape(1, d), wr, br)


def _gather_rows(idx_ref, n_rows, src_hbm, dst_ref, sem):
    def copy(r):
        return pltpu.make_async_copy(src_hbm.at[pl.ds(idx_ref[0, 0, r], 1), :], dst_ref.at[pl.ds(r, 1), :], sem)

    def start(r, c):
        copy(r).start()
        return c

    def wait(r, c):
        copy(r).wait()
        return c

    lax.fori_loop(0, n_rows, start, 0)
    lax.fori_loop(0, n_rows, wait, 0)


def _expert_kernel(be_ref, bv_ref, tok_ref, x_hbm, wg_ref, wu_ref, bg_ref, bu_ref, wd_ref, bd_ref, o_ref,
                   xs_ref, xb_ref, sem, *, tm, n_f):
    b = pl.program_id(0)
    f = pl.program_id(1)
    valid = bv_ref[b] > 0

    @pl.when(jnp.logical_and(valid, f == 0))
    def _():
        _gather_rows(tok_ref, tm, x_hbm, xs_ref, sem)
        xb_ref[...] = xs_ref[...].astype(xb_ref.dtype)

    @pl.when(f == 0)
    def _():
        o_ref[...] = jnp.zeros_like(o_ref)

    @pl.when(valid)
    def _():
        xb = xb_ref[...]
        gate = jnp.dot(xb, wg_ref[0], preferred_element_type=jnp.float32) + bg_ref[0]
        up = jnp.dot(xb, wu_ref[0], preferred_element_type=jnp.float32) + bu_ref[0]
        gate = jnp.minimum(gate, SWIGLU_LIMIT)
        up = jnp.clip(up, -SWIGLU_LIMIT, SWIGLU_LIMIT)
        hh = (up + 1.0) * gate * jax.nn.sigmoid(SWIGLU_ALPHA * gate)
        o_ref[...] += jnp.dot(hh.astype(xb.dtype), wd_ref[0], preferred_element_type=jnp.float32)

    @pl.when(jnp.logical_and(valid, f == n_f - 1))
    def _():
        o_ref[...] += bd_ref[0]


def _experts(xn, blk_expert, blk_valid, tok_rows, w_gu, b_gu, w_down, b_down, tm):
    n_blk = blk_expert.shape[0]
    e, d, two_f = w_gu.shape
    dff = two_f // 2
    tf = 1024 if dff % 1024 == 0 else dff
    n_f = dff // tf

    def fidx(f, bv, b):
        return jnp.where(bv[b] > 0, f, n_f - 1)

    grid_spec = pltpu.PrefetchScalarGridSpec(
        num_scalar_prefetch=2,
        grid=(n_blk, n_f),
        in_specs=[
            pl.BlockSpec((1, 1, tm), lambda b, f, be, bv: (b, 0, 0), memory_space=pltpu.SMEM),
            pl.BlockSpec(memory_space=pl.ANY),
            pl.BlockSpec((1, d, tf), lambda b, f, be, bv: (be[b], 0, fidx(f, bv, b))),
            pl.BlockSpec((1, d, tf), lambda b, f, be, bv: (be[b], 0, n_f + fidx(f, bv, b))),
            pl.BlockSpec((1, 1, tf), lambda b, f, be, bv: (be[b], 0, fidx(f, bv, b))),
            pl.BlockSpec((1, 1, tf), lambda b, f, be, bv: (be[b], 0, n_f + fidx(f, bv, b))),
            pl.BlockSpec((1, tf, d), lambda b, f, be, bv: (be[b], fidx(f, bv, b), 0)),
            pl.BlockSpec((1, 1, d), lambda b, f, be, bv: (be[b], 0, 0)),
        ],
        out_specs=pl.BlockSpec((tm, d), lambda b, f, be, bv: (b, 0)),
        scratch_shapes=[
            pltpu.VMEM((tm, d), jnp.float32),
            pltpu.VMEM((tm, d), jnp.bfloat16),
            pltpu.SemaphoreType.DMA(()),
        ],
    )
    return pl.pallas_call(
        functools.partial(_expert_kernel, tm=tm, n_f=n_f),
        grid_spec=grid_spec,
        out_shape=jax.ShapeDtypeStruct((n_blk * tm, d), jnp.float32),
        compiler_params=_cparams(("arbitrary", "arbitrary")),
        name="experts",
    )(blk_expert, blk_valid, tok_rows.reshape(n_blk, 1, tm), xn, w_gu, w_gu,
      b_gu.reshape(e, 1, two_f), b_gu.reshape(e, 1, two_f), w_down, b_down.reshape(e, 1, d))


def _combine_kernel(pos_ref, h_ref, g_ref, ys_hbm, o_ref, buf_ref, sem, *, tt):
    _gather_rows(pos_ref, TOP_K * tt, ys_hbm, buf_ref, sem)
    g = g_ref[...]
    y = h_ref[...]
    for k in range(TOP_K):
        y = y + buf_ref[k * tt:(k + 1) * tt, :] * g[:, k:k + 1]
    o_ref[...] = y


def _combine(h, gates, pos, ys):
    n, d = h.shape
    tt = _pick(n, (384, 256, 128, 64))
    nt = n // tt
    pos_t = pos.reshape(nt, tt, TOP_K).transpose(0, 2, 1).reshape(nt, 1, TOP_K * tt)
    row = lambda i: (i, 0)
    return pl.pallas_call(
        functools.partial(_combine_kernel, tt=tt),
        grid=(nt,),
        in_specs=[
            pl.BlockSpec((1, 1, TOP_K * tt), lambda i: (i, 0, 0), memory_space=pltpu.SMEM),
            pl.BlockSpec((tt, d), row),
            pl.BlockSpec((tt, LANES), row),
            pl.BlockSpec(memory_space=pl.ANY),
        ],
        out_specs=pl.BlockSpec((tt, d), row),
        out_shape=jax.ShapeDtypeStruct((n, d), jnp.float32),
        scratch_shapes=[pltpu.VMEM((TOP_K * tt, d), jnp.float32), pltpu.SemaphoreType.DMA(())],
        compiler_params=_cparams(("arbitrary",)),
        name="moe_combine",
    )(pos_t, h, gates, ys)


def _moe_layout(top_i, n, tm):
    flat_e = top_i.reshape(-1)
    n_pairs = flat_e.shape[0]
    order = jnp.argsort(flat_e).astype(jnp.int32)
    counts = jnp.sum(flat_e[:, None] == jnp.arange(N_EXPERTS, dtype=jnp.int32)[None, :], axis=0).astype(jnp.int32)
    padded = (counts + tm - 1) // tm * tm
    start = jnp.cumsum(padded) - padded
    cum_excl = jnp.cumsum(counts) - counts
    n_blk = (n_pairs + N_EXPERTS * (tm - 1)) // tm + 1
    blk_end = jnp.cumsum(padded // tm)
    blk = jnp.arange(n_blk, dtype=jnp.int32)
    blk_valid = (blk < blk_end[-1]).astype(jnp.int32)
    blk_expert = jnp.minimum(jnp.searchsorted(blk_end, blk, side="right"), N_EXPERTS - 1).astype(jnp.int32)
    last_valid_e = blk_expert[jnp.maximum(blk_end[-1] - 1, 0)]
    blk_expert = jnp.where(blk_valid > 0, blk_expert, last_valid_e)
    rows = jnp.arange(n_blk * tm, dtype=jnp.int32)
    r_e = blk_expert[rows // tm]
    r_idx = rows - start[r_e]
    r_ok = jnp.logical_and(r_idx < counts[r_e], blk_valid[rows // tm] > 0)
    src = jnp.clip(cum_excl[r_e] + r_idx, 0, n_pairs - 1)
    tok_rows = jnp.where(r_ok, order[src] // TOP_K, 0).astype(jnp.int32)
    inv = jnp.argsort(order).astype(jnp.int32)
    pair_e = flat_e
    pos = (start[pair_e] + inv - cum_excl[pair_e]).astype(jnp.int32)
    return blk_expert, blk_valid, tok_rows, pos.reshape(n, TOP_K)


def _pack_in_proj(w_in, d_model):
    hk = d_model // 2
    hv = d_model
    o = 0
    cuts = {}
    for name, width in (("gq", hk), ("gk", hk), ("gv", hv), ("gr", hv), ("ga", GLA_RANK), ("aq", d_model),
                        ("ak", d_model // GROUP), ("av", d_model // GROUP), ("iq", IDX_HEADS * IDX_DIM),
                        ("ik", IDX_DIM), ("iw", IDX_HEADS), ("za", d_model), ("zb", d_model)):
        cuts[name] = (o, o + width)
        o += width
    assert o == w_in.shape[1]
    col = lambda a, b: w_in[:, cuts[a][0]:cuts[b][1]]
    w1 = col("gq", "gr")
    w2 = col("aq", "av")
    w4 = col("za", "zb")
    k = w_in.shape[0]
    iq = jnp.pad(col("iq", "iq").reshape(k, IDX_HEADS, IDX_DIM), ((0, 0), (0, 0), (0, LANES - IDX_DIM)))
    pad_to = lambda a: jnp.pad(a, ((0, 0), (0, LANES - a.shape[1])))
    w3 = jnp.concatenate([iq.reshape(k, IDX_HEADS * LANES), pad_to(col("ik", "ik")), pad_to(col("ga", "ga")),
                          pad_to(col("iw", "iw"))], axis=1)
    bf = lambda a: a.astype(jnp.bfloat16)
    return bf(w1), bf(w2), bf(w3), bf(w4)


def kernel(x_prompt, x_sample, cache_k, cache_v, cache_kidx, state_gla, norm1_g, w_in, w_alpha2, b_alpha, gla_norm_g, q_norm_g, k_norm_g, w_proj_a, w_proj_b, w_out, norm2_g, w_router, b_router, w_gu, b_gu, w_down, b_down):
    B, T, D = x_prompt.shape
    DB, DS, _ = x_sample.shape
    depth = norm1_g.shape[0]
    NP, NS = B * T, DB * DS
    N = NP + NS
    past = cache_k.shape[2]
    hd = D // ATT_HEADS
    dk, dv = D // (2 * GLA_HEADS), D // GLA_HEADS
    kvw = ATT_KV_HEADS * hd
    bf16 = jnp.bfloat16

    pos = jnp.concatenate([jnp.tile(jnp.arange(T, dtype=jnp.int32), B),
                           jnp.tile(past + jnp.arange(DS, dtype=jnp.int32), DB)])
    x = jnp.concatenate([x_prompt.reshape(NP, D), x_sample.reshape(NS, D)], axis=0)
    outs = {k: [] for k in ("kp", "vp", "kip", "sp", "ks", "vs", "kis", "ss")}

    for l in range(depth):
        w1, w2, w3, w4 = _pack_in_proj(w_in[l], D)
        xn = _rmsnorm(x, norm1_g[l], bf16)
        z1 = _matmul(xn, w1, name="in_proj_gla")
        z2 = _matmul(xn, w2, name="in_proj_attn")
        z3 = _matmul(xn, w3, name="in_proj_idx")
        z4 = _matmul(xn, w4, name="in_proj_gate")

        w2p = jnp.pad(w_alpha2[l], ((0, LANES - GLA_RANK), (0, 0)))
        w2t = w2p.T
        oa_p, S_p = _gla(z1, z3, 0, B, T, w2p, w2t, b_alpha[l], gla_norm_g[l], None, dk, dv)
        oa_s, S_s = _gla(z1, z3, NP, DB, DS, w2p, w2t, b_alpha[l], gla_norm_g[l], state_gla[l], dk, dv)
        o_a = jnp.concatenate([oa_p, oa_s], axis=0)

        q_hm, k_f, k_b, v_b, qc_hm, ki_f, kc, w_pad = _prep(z2, z3, q_norm_g[l], k_norm_g[l], pos, hd)
        v_f = z2[:, ATT_HEADS * hd + kvw:]
        topk_p = min(TOPK_MAX, T // 4)
        tk_p = 1024 if T % 1024 == 0 else (256 if T % 256 == 0 else T)
        ob_p = _sparse_attention(q_hm[:, :NP], qc_hm[:, :NP], w_pad[:NP], kc[:NP].reshape(B, T, -1),
                                 k_b[:NP].reshape(B, T, kvw), v_b[:NP].reshape(B, T, kvw),
                                 B, T, T, 0, topk_p, 128, tk_p)
        lk_s = past + DS
        lk_pad = -(-lk_s // LANES) * LANES
        padk = lambda a: jnp.pad(a, ((0, 0), (0, lk_pad - lk_s), (0, 0)))
        ck_hi = cache_kidx[l].astype(bf16)
        ck_lo = (cache_kidx[l] - ck_hi.astype(jnp.float32)).astype(bf16)
        kc_s = padk(jnp.concatenate([jnp.concatenate([ck_hi, ck_hi, ck_lo, ck_lo], axis=-1),
                                     kc[NP:].reshape(DB, DS, -1)], axis=1))
        kb_s = padk(jnp.concatenate([cache_k[l].reshape(DB, past, kvw).astype(bf16),
                                     k_b[NP:].reshape(DB, DS, kvw)], axis=1))
        vb_s = padk(jnp.concatenate([cache_v[l].reshape(DB, past, kvw).astype(bf16),
                                     v_b[NP:].reshape(DB, DS, kvw)], axis=1))
        topk_s = min(TOPK_MAX, lk_s // 4)
        ob_s = _sparse_attention(q_hm[:, NP:], qc_hm[:, NP:], w_pad[NP:], kc_s, kb_s, vb_s,
                                 DB, DS, lk_s, past, topk_s, DS, lk_pad)
        o_b = jnp.concatenate([ob_p, ob_s], axis=0)

        mixed = _merge(o_a, o_b, w_proj_a[l].astype(bf16), w_proj_b[l].astype(bf16), z4)
        h = _resid_matmul(x, mixed, w_out[l].astype(bf16))

        xn2, top_i, gates = _router(h, norm2_g[l], w_router[l], b_router[l])
        tm_e = 512 if N >= 8192 else 64
        blk_e, blk_v, tok_rows, pos_rows = _moe_layout(top_i[:, :TOP_K], N, tm_e)
        ys = _experts(xn2, blk_e, blk_v, tok_rows, w_gu[l].astype(bf16), b_gu[l], w_down[l].astype(bf16),
                      b_down[l], tm_e)
        x = _combine(h, gates, pos_rows, ys)

        outs["kp"].append(k_f[:NP].reshape(B, T, ATT_KV_HEADS, hd))
        outs["vp"].append(v_f[:NP].reshape(B, T, ATT_KV_HEADS, hd))
        outs["kip"].append(ki_f[:NP].reshape(B, T, IDX_DIM))
        outs["sp"].append(S_p)
        outs["ks"].append(k_f[NP:].reshape(DB, DS, ATT_KV_HEADS, hd))
        outs["vs"].append(v_f[NP:].reshape(DB, DS, ATT_KV_HEADS, hd))
        outs["kis"].append(ki_f[NP:].reshape(DB, DS, IDX_DIM))
        outs["ss"].append(S_s)

    st = lambda k: jnp.stack(outs[k], axis=0)
    return (x[:NP].reshape(B, T, D), x[NP:].reshape(DB, DS, D), st("kp"), st("vp"), st("kip"), st("sp"),
            st("ks"), st("vs"), st("kis"), st("ss"))
```

```python
import functools
import math

import jax
import jax.numpy as jnp
from jax import lax
from jax.experimental import pallas as pl
from jax.experimental.pallas import tpu as pltpu

CHUNK = 64
GLA_HEADS = 4
GLA_RANK = 16
GLA_TAU = 16.0
ATT_HEADS = 16
ATT_KV_HEADS = 4
GROUP = ATT_HEADS // ATT_KV_HEADS
ROPE_THETA = 500000.0
IDX_HEADS = 8
IDX_DIM = 64
TOPK_MAX = 256
N_EXPERTS = 32
TOP_K = 4
SWIGLU_LIMIT = 7.0
SWIGLU_ALPHA = 1.702
EPS = 1e-6

LANES = 128
VMEM_LIMIT = 56 * 1024 * 1024
INT_MIN = -(2 ** 31)
ATTN_ROW_CHUNK = 16
SCORE_COLS = 256
LOG2E = 1.4426950408889634
NEG_INIT = -1e30

_HIGHEST = lax.Precision.HIGHEST
_NT = (((1,), (1,)), ((), ()))
_TN = (((0,), (0,)), ((), ()))


def _pick(n, cands):
    for c in cands:
        if n % c == 0:
            return c
    raise ValueError(f"no tile in {cands} divides {n}")


def _cparams(sem):
    return pltpu.CompilerParams(dimension_semantics=sem, vmem_limit_bytes=VMEM_LIMIT)


def _rmsnorm_kernel(x_ref, g_ref, o_ref):
    x = x_ref[...]
    y = x * lax.rsqrt(jnp.mean(x * x, axis=-1, keepdims=True) + EPS) * g_ref[...]
    o_ref[...] = y.astype(o_ref.dtype)


def _rmsnorm(x, g, out_dtype):
    n, d = x.shape
    tm = _pick(n, (768, 512, 256, 128, 64))
    return pl.pallas_call(
        _rmsnorm_kernel,
        grid=(n // tm,),
        in_specs=[pl.BlockSpec((tm, d), lambda i: (i, 0)), pl.BlockSpec((1, d), lambda i: (0, 0))],
        out_specs=pl.BlockSpec((tm, d), lambda i: (i, 0)),
        out_shape=jax.ShapeDtypeStruct((n, d), out_dtype),
        compiler_params=_cparams(("parallel",)),
        name="rmsnorm",
    )(x, g.reshape(1, d))


def _mm_kernel(a_ref, w_ref, o_ref):
    o_ref[...] = jnp.dot(a_ref[...], w_ref[...], preferred_element_type=jnp.float32).astype(o_ref.dtype)


def _matmul(a, w, out_dtype=jnp.float32, name="matmul"):
    m, k = a.shape
    _, n = w.shape
    tm = _pick(m, (768, 512, 256, 128, 64))
    tn = 1024 if n % 1024 == 0 else (512 if n % 512 == 0 else n)
    return pl.pallas_call(
        _mm_kernel,
        grid=(m // tm, n // tn),
        in_specs=[pl.BlockSpec((tm, k), lambda i, j: (i, 0)), pl.BlockSpec((k, tn), lambda i, j: (0, j))],
        out_specs=pl.BlockSpec((tm, tn), lambda i, j: (i, j)),
        out_shape=jax.ShapeDtypeStruct((m, n), out_dtype),
        compiler_params=_cparams(("parallel", "arbitrary")),
        name=name,
    )(a, w)


def _merge_kernel(oa_ref, ob_ref, wa_ref, wb_ref, za_ref, zb_ref, o_ref):
    pa = jnp.dot(oa_ref[...], wa_ref[...], preferred_element_type=jnp.float32)
    pb = jnp.dot(ob_ref[...], wb_ref[...], preferred_element_type=jnp.float32)
    o_ref[...] = (jax.nn.sigmoid(za_ref[...]) * pa + jax.nn.sigmoid(zb_ref[...]) * pb).astype(o_ref.dtype)


def _merge(o_a, o_b, wa, wb, z4):
    m, k = o_a.shape
    n = wa.shape[1]
    tm = _pick(m, (768, 512, 256, 128, 64))
    tn = 512
    nj = n // tn
    return pl.pallas_call(
        _merge_kernel,
        grid=(m // tm, nj),
        in_specs=[
            pl.BlockSpec((tm, k), lambda i, j: (i, 0)),
            pl.BlockSpec((tm, k), lambda i, j: (i, 0)),
            pl.BlockSpec((k, tn), lambda i, j: (0, j)),
            pl.BlockSpec((k, tn), lambda i, j: (0, j)),
            pl.BlockSpec((tm, tn), lambda i, j: (i, j)),
            pl.BlockSpec((tm, tn), lambda i, j: (i, j + nj)),
        ],
        out_specs=pl.BlockSpec((tm, tn), lambda i, j: (i, j)),
        out_shape=jax.ShapeDtypeStruct((m, n), jnp.bfloat16),
        compiler_params=_cparams(("parallel", "arbitrary")),
        name="merge",
    )(o_a, o_b, wa, wb, z4, z4)


def _resid_mm_kernel(x_ref, a_ref, w_ref, o_ref):
    o_ref[...] = x_ref[...] + jnp.dot(a_ref[...], w_ref[...], preferred_element_type=jnp.float32)


def _resid_matmul(x, a, w):
    m, k = a.shape
    n = w.shape[1]
    tm = _pick(m, (768, 512, 256, 128, 64))
    tn = 1024 if n % 1024 == 0 else n
    return pl.pallas_call(
        _resid_mm_kernel,
        grid=(m // tm, n // tn),
        in_specs=[
            pl.BlockSpec((tm, tn), lambda i, j: (i, j)),
            pl.BlockSpec((tm, k), lambda i, j: (i, 0)),
            pl.BlockSpec((k, tn), lambda i, j: (0, j)),
        ],
        out_specs=pl.BlockSpec((tm, tn), lambda i, j: (i, j)),
        out_shape=jax.ShapeDtypeStruct((m, n), jnp.float32),
        compiler_params=_cparams(("parallel", "arbitrary")),
        name="out_proj",
    )(x, a, w)


def _log_sigmoid(x):
    return jnp.minimum(x, 0.0) - jnp.log1p(jnp.exp(-jnp.abs(x)))


def _gla_kernel(*refs, c, dk, dv, has_init):
    if has_init:
        (z1_ref, ga_ref, w2_ref, w2t_ref, b_ref, bcol_ref, gn_ref, s0_ref, o_ref, sout_ref, s_scr) = refs
    else:
        (z1_ref, ga_ref, w2_ref, w2t_ref, b_ref, bcol_ref, gn_ref, o_ref, sout_ref, s_scr) = refs
        s0_ref = None
    ci = pl.program_id(1)
    hk = GLA_HEADS * dk

    @pl.when(ci == 0)
    def _():
        if has_init:
            s_scr[...] = s0_ref[0]
        else:
            s_scr[...] = jnp.zeros_like(s_scr)

    ga = ga_ref[...]
    g = _log_sigmoid(jnp.dot(ga, w2_ref[...], precision=_HIGHEST,
                             preferred_element_type=jnp.float32) + b_ref[...]) / GLA_TAU
    row = lax.broadcasted_iota(jnp.int32, (c, c), 0)
    col = lax.broadcasted_iota(jnp.int32, (c, c), 1)
    causal = col <= row
    b = jnp.dot(causal.astype(jnp.float32), g, precision=_HIGHEST, preferred_element_type=jnp.float32)
    gt = _log_sigmoid(lax.dot_general(w2t_ref[...], ga, _NT, precision=_HIGHEST,
                                      preferred_element_type=jnp.float32) + bcol_ref[...]) / GLA_TAU
    bend_col = jnp.sum(gt, axis=1, keepdims=True)
    scale = dk ** -0.5
    for h in range(GLA_HEADS):
        bh = b[:, h * dk:(h + 1) * dk]
        q = z1_ref[:, h * dk:(h + 1) * dk] * scale
        k = z1_ref[:, hk + h * dk: hk + (h + 1) * dk]
        v = z1_ref[:, 2 * hk + h * dv: 2 * hk + (h + 1) * dv].astype(jnp.bfloat16)
        r = z1_ref[:, 2 * hk + GLA_HEADS * dv + h * dv: 2 * hk + GLA_HEADS * dv + (h + 1) * dv]
        qe = (q * jnp.exp(bh)).astype(jnp.bfloat16)
        ke = (k * jnp.exp(-bh)).astype(jnp.bfloat16)
        a = lax.dot_general(qe, ke, _NT, preferred_element_type=jnp.float32)
        a = jnp.where(causal, a, 0.0).astype(jnp.bfloat16)
        s = s_scr[h]
        o = (jnp.dot(a, v, preferred_element_type=jnp.float32)
             + jnp.dot(qe, s.astype(jnp.bfloat16), preferred_element_type=jnp.float32))
        b_end = bh[c - 1:c, :]
        kd = (k * jnp.exp(b_end - bh)).astype(jnp.bfloat16)
        s_scr[h] = (jnp.exp(bend_col[h * dk:(h + 1) * dk, :]) * s
                    + lax.dot_general(kd, v, _TN, preferred_element_type=jnp.float32))
        on = o * lax.rsqrt(jnp.mean(o * o, axis=-1, keepdims=True) + EPS) * gn_ref[...]
        o_ref[:, h * dv:(h + 1) * dv] = (on * (r * jax.nn.sigmoid(r))).astype(o_ref.dtype)

    @pl.when(ci == pl.num_programs(1) - 1)
    def _():
        sout_ref[0] = s_scr[...]


def _gla(z1, z3, row0, n_seq, seq_len, w2p, w2t, b_alpha, gn, s0, dk, dv):
    c = CHUNK if seq_len % CHUNK == 0 else seq_len
    nc = seq_len // c
    hk = GLA_HEADS * dk
    blk0 = row0 // c
    ga_blk = (IDX_HEADS + 1)
    has_init = s0 is not None
    rows = n_seq * seq_len
    in_specs = [
        pl.BlockSpec((c, z1.shape[1]), lambda s, i: (blk0 + s * nc + i, 0)),
        pl.BlockSpec((c, LANES), lambda s, i: (blk0 + s * nc + i, ga_blk)),
        pl.BlockSpec((LANES, hk), lambda s, i: (0, 0)),
        pl.BlockSpec((hk, LANES), lambda s, i: (0, 0)),
        pl.BlockSpec((1, hk), lambda s, i: (0, 0)),
        pl.BlockSpec((hk, 1), lambda s, i: (0, 0)),
        pl.BlockSpec((1, dv), lambda s, i: (0, 0)),
    ]
    args = [z1, z3, w2p, w2t, b_alpha.reshape(1, hk), b_alpha.reshape(hk, 1), gn.reshape(1, dv)]
    if has_init:
        in_specs.append(pl.BlockSpec((1, GLA_HEADS, dk, dv), lambda s, i: (s, 0, 0, 0)))
        args.append(s0)
    return pl.pallas_call(
        functools.partial(_gla_kernel, c=c, dk=dk, dv=dv, has_init=has_init),
        grid=(n_seq, nc),
        in_specs=in_specs,
        out_specs=[
            pl.BlockSpec((c, GLA_HEADS * dv), lambda s, i: (s * nc + i, 0)),
            pl.BlockSpec((1, GLA_HEADS, dk, dv), lambda s, i: (s, 0, 0, 0)),
        ],
        out_shape=[
            jax.ShapeDtypeStruct((rows, GLA_HEADS * dv), jnp.bfloat16),
            jax.ShapeDtypeStruct((n_seq, GLA_HEADS, dk, dv), jnp.float32),
        ],
        scratch_shapes=[pltpu.VMEM((GLA_HEADS, dk, dv), jnp.float32)],
        compiler_params=_cparams(("parallel", "arbitrary")),
        name="gla_init" if has_init else "gla",
    )(*args)


def _rope(x, cos, sin, half):
    lane = lax.broadcasted_iota(jnp.int32, x.shape, 1)
    partner = jnp.where(lane < half, pltpu.roll(x, LANES - half, 1), pltpu.roll(x, half, 1))
    return x * cos + partner * sin


def _split_hi_lo(x):
    hi = x.astype(jnp.bfloat16).astype(jnp.float32)
    lo = (x - hi).astype(jnp.bfloat16).astype(jnp.float32)
    return hi, lo


def _prep_kernel(z2_ref, z3_ref, qg_ref, kg_ref, ca_ref, sa_ref, ci_ref, si_ref,
                 q_ref, kf_ref, kb_ref, vb_ref, qc_ref, kif_ref, kc_ref, w_ref, *, hd):
    ca, sa, ci, si = ca_ref[...], sa_ref[...], ci_ref[...], si_ref[...]
    qscale = hd ** -0.5 * LOG2E

    def headnorm(x, g):
        return x * lax.rsqrt(jnp.mean(x * x, axis=-1, keepdims=True) + EPS) * g

    for h in range(ATT_HEADS):
        x = headnorm(z2_ref[:, h * hd:(h + 1) * hd], qg_ref[...])
        q_ref[h] = (_rope(x, ca, sa, hd // 8) * qscale).astype(q_ref.dtype)
    k0 = ATT_HEADS * hd
    for h in range(ATT_KV_HEADS):
        x = headnorm(z2_ref[:, k0 + h * hd:k0 + (h + 1) * hd], kg_ref[...])
        y = _rope(x, ca, sa, hd // 8)
        kf_ref[:, h * hd:(h + 1) * hd] = y
        kb_ref[:, h * hd:(h + 1) * hd] = y.astype(kb_ref.dtype)
    v0 = k0 + ATT_KV_HEADS * hd
    vb_ref[...] = z2_ref[:, v0:v0 + ATT_KV_HEADS * hd].astype(vb_ref.dtype)
    for h in range(IDX_HEADS):
        y = _rope(z3_ref[:, h * LANES:(h + 1) * LANES], ci, si, IDX_DIM // 8)
        hi, lo = _split_hi_lo(y)
        t = (hi + pltpu.roll(lo, IDX_DIM, 1)).astype(qc_ref.dtype)
        qc_ref[h] = jnp.concatenate([t, t], axis=1)
    y = _rope(z3_ref[:, IDX_HEADS * LANES:(IDX_HEADS + 1) * LANES], ci, si, IDX_DIM // 8)
    kif_ref[...] = y[:, :IDX_DIM]
    hi, lo = _split_hi_lo(y)
    kc_ref[...] = jnp.concatenate([hi + pltpu.roll(hi, IDX_DIM, 1), lo + pltpu.roll(lo, IDX_DIM, 1)],
                                  axis=1).astype(kc_ref.dtype)
    w_ref[...] = z3_ref[:, (IDX_HEADS + 2) * LANES:(IDX_HEADS + 3) * LANES] * ((IDX_HEADS * IDX_DIM) ** -0.5)


def _rope_tables(pos, rot):
    half = rot // 2
    inv = ROPE_THETA ** (-2.0 * jnp.arange(half, dtype=jnp.float32) / rot)
    ang = pos.astype(jnp.float32)[:, None] * inv[None, :]
    cos, sin = jnp.cos(ang), jnp.sin(ang)
    n = pos.shape[0]
    ones = jnp.ones((n, LANES - rot), jnp.float32)
    zeros = jnp.zeros((n, LANES - rot), jnp.float32)
    return (jnp.concatenate([cos, cos, ones], axis=1), jnp.concatenate([-sin, sin, zeros], axis=1))


def _prep(z2, z3, q_norm_g, k_norm_g, pos, hd):
    n = z2.shape[0]
    tr = _pick(n, (256, 128, 64))
    ca, sa = _rope_tables(pos, hd // 4)
    ci, si = _rope_tables(pos, IDX_DIM // 4)
    row = lambda i: (i, 0)
    tab = pl.BlockSpec((tr, LANES), row)
    kvw = ATT_KV_HEADS * hd
    return pl.pallas_call(
        functools.partial(_prep_kernel, hd=hd),
        grid=(n // tr,),
        in_specs=[
            pl.BlockSpec((tr, z2.shape[1]), row),
            pl.BlockSpec((tr, z3.shape[1]), row),
            pl.BlockSpec((1, hd), lambda i: (0, 0)),
            pl.BlockSpec((1, hd), lambda i: (0, 0)),
            tab, tab, tab, tab,
        ],
        out_specs=[
            pl.BlockSpec((ATT_HEADS, tr, hd), lambda i: (0, i, 0)),
            pl.BlockSpec((tr, kvw), row),
            pl.BlockSpec((tr, kvw), row),
            pl.BlockSpec((tr, kvw), row),
            pl.BlockSpec((IDX_HEADS, tr, 4 * IDX_DIM), lambda i: (0, i, 0)),
            pl.BlockSpec((tr, IDX_DIM), row),
            pl.BlockSpec((tr, 4 * IDX_DIM), row),
            pl.BlockSpec((tr, LANES), row),
        ],
        out_shape=[
            jax.ShapeDtypeStruct((ATT_HEADS, n, hd), jnp.bfloat16),
            jax.ShapeDtypeStruct((n, kvw), jnp.float32),
            jax.ShapeDtypeStruct((n, kvw), jnp.bfloat16),
            jax.ShapeDtypeStruct((n, kvw), jnp.bfloat16),
            jax.ShapeDtypeStruct((IDX_HEADS, n, 4 * IDX_DIM), jnp.bfloat16),
            jax.ShapeDtypeStruct((n, IDX_DIM), jnp.float32),
            jax.ShapeDtypeStruct((n, 4 * IDX_DIM), jnp.bfloat16),
            jax.ShapeDtypeStruct((n, LANES), jnp.float32),
        ],
        compiler_params=_cparams(("parallel",)),
        name="attn_prep",
    )(z2, z3, q_norm_g.reshape(1, hd), k_norm_g.reshape(1, hd), ca, sa, ci, si)


def _needed_tiles(i, tq, tk, lk_true, pos_off):
    last = i * tq + pos_off + tq - 1
    max_limit = jnp.minimum((last // CHUNK + 1) * CHUNK, lk_true)
    return (max_limit + tk - 1) // tk


def _attn_kernel(q_ref, qc_ref, w_ref, kc_ref, k_ref, v_ref, o_ref,
                 keys_ref, hi_ref, lo_ref, thr_ref, bias_ref, s_ref, p_ref, alpha_ref, m_ref, l_ref, acc_ref,
                 *, tq, tk, n_kt, lk_true, pos_off, topk, nbits, hd):
    i = pl.program_id(1)
    kt = pl.program_id(2)
    n_need = _needed_tiles(i, tq, tk, lk_true, pos_off)
    n_sub = tk // LANES
    rows = GROUP * tq
    rc = min(ATTN_ROW_CHUNK, tq)

    def tile(j):
        return pl.ds(pl.multiple_of(j * tk, tk), tk)

    def lanes(j, c):
        return pl.ds(pl.multiple_of(j * tk + c * LANES, LANES), LANES)

    @pl.when(kt == 0)
    def _select():
        rowi = lax.broadcasted_iota(jnp.int32, (tq, 1), 0)
        limit = jnp.minimum(((i * tq + pos_off + rowi) // CHUNK + 1) * CHUNK, lk_true)
        k_eff = jnp.minimum(topk, limit).astype(jnp.float32)
        w = w_ref[...]
        qc_all = qc_ref[...].reshape(IDX_HEADS * tq, 4 * IDX_DIM)
        cw = SCORE_COLS if tk % SCORE_COLS == 0 else tk

        def score_body(j, carry):
            for cs in range(tk // cw):
                cols = pl.ds(pl.multiple_of(j * tk + cs * cw, cw), cw)
                s_all = lax.dot_general(qc_all, kc_ref[0, cols, :], _NT, preferred_element_type=jnp.float32)
                acc = jnp.zeros((tq, cw), jnp.float32)
                for h in range(IDX_HEADS):
                    acc = acc + jnp.maximum(s_all[h * tq:(h + 1) * tq], 0.0) * w[:, h:h + 1]
                bits = pltpu.bitcast(acc, jnp.int32)
                sgn = bits >> 31
                key = (bits ^ (sgn & 0x7FFFFFFF)) - sgn
                col = j * tk + cs * cw + lax.broadcasted_iota(jnp.int32, (tq, cw), 1)
                key = jnp.where(col < limit, key, INT_MIN)
                keys_ref[:, cols] = key
                hi_ref[:, cols] = (key >> 16).astype(jnp.int16)
            return carry

        lax.fori_loop(0, n_need, score_body, 0)

        def count(pred):
            def body(j, acc):
                for c in range(n_sub):
                    kk = keys_ref[:, lanes(j, c)]
                    col = j * tk + c * LANES + lax.broadcasted_iota(jnp.int32, (tq, LANES), 1)
                    acc = acc + jnp.where(pred(kk, col), 1.0, 0.0)
                return acc
            acc = lax.fori_loop(0, n_need, body, jnp.zeros((tq, LANES), jnp.float32))
            return jnp.sum(acc, axis=1, keepdims=True)

        def count16(ref, pred):
            one = jnp.ones((tq, LANES), jnp.int16)
            zero = jnp.zeros((tq, LANES), jnp.int16)

            def body(j, acc):
                for c in range(n_sub):
                    acc = acc + jnp.where(pred(ref[:, lanes(j, c)]), one, zero)
                return acc
            acc = lax.fori_loop(0, n_need, body, zero)
            return jnp.sum(acc.astype(jnp.float32), axis=1, keepdims=True)

        def as16(t):
            return jnp.broadcast_to(t, (tq, LANES)).astype(jnp.int16)

        def search16(ref, target):
            def bit_body(b, cand):
                t = cand + lax.shift_left(jnp.int32(1), 15 - b)
                t16 = as16(t)
                c = count16(ref, lambda kk: kk >= t16)
                return jnp.where(c >= target, t, cand)
            return lax.fori_loop(0, 16, bit_body, jnp.full((tq, 1), -32768, jnp.int32))

        thr_hi = search16(hi_ref, k_eff)
        thr_hi16 = as16(thr_hi)
        k_lo = k_eff - count16(hi_ref, lambda kk: kk > thr_hi16)

        def lo_body(j, carry):
            key = keys_ref[:, tile(j)]
            lo = jnp.where((key >> 16) == thr_hi, (key & 0xFFFF) - 32768, -32768)
            lo_ref[:, tile(j)] = lo.astype(jnp.int16)
            return carry

        lax.fori_loop(0, n_need, lo_body, 0)
        thr_lo = search16(lo_ref, k_lo)
        thr = thr_hi * 65536 + (thr_lo + 32768)
        thr_ref[...] = thr
        c_gt = count(lambda kk, col: kk > thr)
        c_ge = count(lambda kk, col: kk >= thr)
        need = k_eff - c_gt
        excess = jnp.max(c_ge - k_eff) > 0.5

        @pl.when(excess)
        def _ties():
            def idx_body(b, cut):
                t = cut + lax.shift_left(jnp.int32(1), nbits - 1 - b)
                c = count(lambda kk, col: jnp.where(kk == thr, col, t) < t)
                return jnp.where(c <= need, t, cut)

            cut = lax.fori_loop(0, nbits, idx_body, jnp.zeros((tq, 1), jnp.int32))

            def fix_body(j, carry):
                kk = keys_ref[:, tile(j)]
                col = j * tk + lax.broadcasted_iota(jnp.int32, (tq, tk), 1)
                drop = jnp.where(kk == thr, col, -1) >= cut
                keys_ref[:, tile(j)] = jnp.where(drop, INT_MIN, kk)
                return carry

            lax.fori_loop(0, n_need, fix_body, 0)

        m_ref[...] = jnp.full(m_ref.shape, NEG_INIT, jnp.float32)
        l_ref[...] = jnp.zeros(l_ref.shape, jnp.float32)
        acc_ref[...] = jnp.zeros(acc_ref.shape, jnp.float32)

    @pl.when(kt < n_need)
    def _attend():
        bias_ref[...] = jnp.where(keys_ref[:, tile(kt)] >= thr_ref[...], 0.0, -jnp.inf)

        def qk(kvh):
            q4 = q_ref[kvh * GROUP:(kvh + 1) * GROUP].reshape(rows, hd)
            kb = k_ref[0, :, kvh * hd:(kvh + 1) * hd]
            s_ref[kvh % 2] = lax.dot_general(q4, kb, _NT, preferred_element_type=jnp.float32)

        qk(0)
        for kvh in range(ATT_KV_HEADS):
            if kvh + 1 < ATT_KV_HEADS:
                qk(kvh + 1)
            sb = kvh % 2
            for c in range(rows // rc):
                r0 = c * rc
                rb = (c % (tq // rc)) * rc
                t = s_ref[sb, r0:r0 + rc, :] + bias_ref[rb:rb + rc, :]
                m_prev = m_ref[kvh, r0:r0 + rc, :]
                m_new = jnp.maximum(m_prev, jnp.max(t, axis=1, keepdims=True))
                p = jnp.exp2(t - m_new)
                alpha = jnp.exp2(m_prev - m_new)
                psum = p[:, 0:LANES]
                for cc in range(1, n_sub):
                    psum = psum + p[:, cc * LANES:(cc + 1) * LANES]
                l_ref[kvh, r0:r0 + rc, :] = alpha * l_ref[kvh, r0:r0 + rc, :] + psum
                p_ref[sb, r0:r0 + rc, :] = p.astype(p_ref.dtype)
                alpha_ref[sb, r0:r0 + rc, :] = alpha
                m_ref[kvh, r0:r0 + rc, :] = m_new
            vb = v_ref[0, :, kvh * hd:(kvh + 1) * hd]
            acc_ref[kvh] = alpha_ref[sb] * acc_ref[kvh] + jnp.dot(p_ref[sb], vb, preferred_element_type=jnp.float32)

    @pl.when(kt == n_kt - 1)
    def _finish():
        for kvh in range(ATT_KV_HEADS):
            o = acc_ref[kvh] / jnp.sum(l_ref[kvh], axis=1, keepdims=True)
            for g in range(GROUP):
                h = kvh * GROUP + g
                o_ref[:, h * hd:(h + 1) * hd] = o[g * tq:(g + 1) * tq].astype(o_ref.dtype)


def _sparse_attention(q_hm, qcat_hm, w_pad, kcat, kb, vb, n_grp, tq_total, lk_true, pos_off, topk, tq, tk):
    hd = q_hm.shape[-1]
    lk_pad = kb.shape[1]
    nq = tq_total // tq
    n_kt = lk_pad // tk
    nbits = max(1, math.ceil(math.log2(lk_pad + 1)))
    need = functools.partial(_needed_tiles, tq=tq, tk=tk, lk_true=lk_true, pos_off=pos_off)
    kv_map = lambda g, i, kt: (g, jnp.minimum(kt, need(i) - 1), 0)
    qrow = lambda g, i, kt: (g * nq + i, 0)
    kern = functools.partial(_attn_kernel, tq=tq, tk=tk, n_kt=n_kt, lk_true=lk_true, pos_off=pos_off,
                             topk=topk, nbits=nbits, hd=hd)
    return pl.pallas_call(
        kern,
        grid=(n_grp, nq, n_kt),
        in_specs=[
            pl.BlockSpec((ATT_HEADS, tq, hd), lambda g, i, kt: (0, g * nq + i, 0)),
            pl.BlockSpec((IDX_HEADS, tq, 4 * IDX_DIM), lambda g, i, kt: (0, g * nq + i, 0)),
            pl.BlockSpec((tq, LANES), qrow),
            pl.BlockSpec((1, lk_pad, 4 * IDX_DIM), lambda g, i, kt: (g, 0, 0)),
            pl.BlockSpec((1, tk, ATT_KV_HEADS * hd), kv_map),
            pl.BlockSpec((1, tk, ATT_KV_HEADS * hd), kv_map),
        ],
        out_specs=pl.BlockSpec((tq, ATT_HEADS * hd), qrow),
        out_shape=jax.ShapeDtypeStruct((n_grp * tq_total, ATT_HEADS * hd), jnp.bfloat16),
        scratch_shapes=[
            pltpu.VMEM((tq, lk_pad), jnp.int32),
            pltpu.VMEM((tq, lk_pad), jnp.int16),
            pltpu.VMEM((tq, lk_pad), jnp.int16),
            pltpu.VMEM((tq, 1), jnp.int32),
            pltpu.VMEM((tq, tk), jnp.float32),
            pltpu.VMEM((2, GROUP * tq, tk), jnp.float32),
            pltpu.VMEM((2, GROUP * tq, tk), jnp.bfloat16),
            pltpu.VMEM((2, GROUP * tq, 1), jnp.float32),
            pltpu.VMEM((ATT_KV_HEADS, GROUP * tq, 1), jnp.float32),
            pltpu.VMEM((ATT_KV_HEADS, GROUP * tq, LANES), jnp.float32),
            pltpu.VMEM((ATT_KV_HEADS, GROUP * tq, hd), jnp.float32),
        ],
        compiler_params=_cparams(("parallel", "arbitrary", "arbitrary")),
        name="sparse_attn",
    )(q_hm, qcat_hm, w_pad, kcat, kb, vb)


def _router_kernel(h_ref, g_ref, wr_ref, br_ref, xn_ref, ti_ref, tg_ref):
    h = h_ref[...]
    xn = h * lax.rsqrt(jnp.mean(h * h, axis=-1, keepdims=True) + EPS) * g_ref[...]
    xn_ref[...] = xn
    logits = jnp.dot(xn, wr_ref[...], precision=_HIGHEST, preferred_element_type=jnp.float32) + br_ref[...]
    lane = lax.broadcasted_iota(jnp.int32, logits.shape, 1)
    cur = logits
    vals, idxs = [], []
    for _ in range(TOP_K):
        m = jnp.max(cur, axis=1, keepdims=True)
        idx = jnp.min(jnp.where(cur == m, lane, LANES), axis=1, keepdims=True)
        vals.append(m)
        idxs.append(idx)
        cur = jnp.where(lane == idx, -jnp.inf, cur)
    es = [jnp.exp(v - vals[0]) for v in vals]
    denom = es[0] + es[1] + es[2] + es[3]
    ti = jnp.zeros(logits.shape, jnp.int32)
    tg = jnp.zeros(logits.shape, jnp.float32)
    for k in range(TOP_K):
        ti = jnp.where(lane == k, idxs[k], ti)
        tg = jnp.where(lane == k, es[k] / denom, tg)
    ti_ref[...] = ti
    tg_ref[...] = tg


def _router(h, g, w_router, b_router):
    n, d = h.shape
    tm = _pick(n, (768, 512, 256, 128, 64))
    wr = jnp.pad(w_router, ((0, 0), (0, LANES - N_EXPERTS)))
    br = jnp.pad(b_router, (0, LANES - N_EXPERTS), constant_values=NEG_INIT).reshape(1, LANES)
    row = lambda i: (i, 0)
    return pl.pallas_call(
        _router_kernel,
        grid=(n // tm,),
        in_specs=[
            pl.BlockSpec((tm, d), row),
            pl.BlockSpec((1, d), lambda i: (0, 0)),
            pl.BlockSpec((d, LANES), lambda i: (0, 0)),
            pl.BlockSpec((1, LANES), lambda i: (0, 0)),
        ],
        out_specs=[pl.BlockSpec((tm, d), row), pl.BlockSpec((tm, LANES), row), pl.BlockSpec((tm, LANES), row)],
        out_shape=[
            jax.ShapeDtypeStruct((n, d), jnp.float32),
            jax.ShapeDtypeStruct((n, LANES), jnp.int32),
            jax.ShapeDtypeStruct((n, LANES), jnp.float32),
        ],
        compiler_params=_cparams(("parallel",)),
        name="router",
    )(h, g.reshape(1, d), wr, br)


def _start_row_gather(idx_ref, n_rows, src_hbm, dst_ref, sem):
    def start(r, c):
        pltpu.make_async_copy(src_hbm.at[pl.ds(idx_ref[0, 0, r], 1), :], dst_ref.at[pl.ds(r, 1), :], sem).start()
        return c

    lax.fori_loop(0, n_rows, start, 0)


def _wait_row_gather(n_rows, src_hbm, dst_ref, sem):
    pltpu.make_async_copy(src_hbm.at[pl.ds(0, n_rows), :], dst_ref, sem).wait()


def _expert_kernel(be_ref, bv_ref, tok_ref, tok_next_ref, x_hbm, wg_ref, wu_ref, bg_ref, bu_ref, wd_ref, bd_ref,
                   o_ref, xs_ref, xb_ref, sem, *, tm, n_f, n_blk):
    b = pl.program_id(0)
    f = pl.program_id(1)
    valid = bv_ref[b] > 0
    slot = b % 2

    @pl.when(f == 0)
    def _():
        @pl.when(jnp.logical_and(b == 0, valid))
        def _():
            _start_row_gather(tok_ref, tm, x_hbm, xs_ref.at[0], sem.at[0])

        nxt = jnp.minimum(b + 1, n_blk - 1)

        @pl.when(jnp.logical_and(b + 1 < n_blk, bv_ref[nxt] > 0))
        def _():
            _start_row_gather(tok_next_ref, tm, x_hbm, xs_ref.at[1 - slot], sem.at[1 - slot])

        @pl.when(valid)
        def _():
            _wait_row_gather(tm, x_hbm, xs_ref.at[slot], sem.at[slot])
            xb_ref[...] = xs_ref[slot].astype(xb_ref.dtype)

        o_ref[...] = jnp.zeros_like(o_ref)

    @pl.when(valid)
    def _():
        xb = xb_ref[...]
        gate = jnp.dot(xb, wg_ref[0], preferred_element_type=jnp.float32) + bg_ref[0]
        up = jnp.dot(xb, wu_ref[0], preferred_element_type=jnp.float32) + bu_ref[0]
        gate = jnp.minimum(gate, SWIGLU_LIMIT)
        up = jnp.clip(up, -SWIGLU_LIMIT, SWIGLU_LIMIT)
        hh = (up + 1.0) * gate * jax.nn.sigmoid(SWIGLU_ALPHA * gate)
        o_ref[...] += jnp.dot(hh.astype(xb.dtype), wd_ref[0], preferred_element_type=jnp.float32)

    @pl.when(jnp.logical_and(valid, f == n_f - 1))
    def _():
        o_ref[...] += bd_ref[0]


def _experts(xn, blk_expert, blk_valid, tok_rows, w_gu, b_gu, w_down, b_down, tm):
    n_blk = blk_expert.shape[0]
    e, d, two_f = w_gu.shape
    dff = two_f // 2
    tf = 1024 if dff % 1024 == 0 else dff
    n_f = dff // tf

    def fidx(f, bv, b):
        return jnp.where(bv[b] > 0, f, n_f - 1)

    grid_spec = pltpu.PrefetchScalarGridSpec(
        num_scalar_prefetch=2,
        grid=(n_blk, n_f),
        in_specs=[
            pl.BlockSpec((1, 1, tm), lambda b, f, be, bv: (b, 0, 0), memory_space=pltpu.SMEM),
            pl.BlockSpec((1, 1, tm), lambda b, f, be, bv: (jnp.minimum(b + 1, n_blk - 1), 0, 0),
                         memory_space=pltpu.SMEM),
            pl.BlockSpec(memory_space=pl.ANY),
            pl.BlockSpec((1, d, tf), lambda b, f, be, bv: (be[b], 0, fidx(f, bv, b))),
            pl.BlockSpec((1, d, tf), lambda b, f, be, bv: (be[b], 0, n_f + fidx(f, bv, b))),
            pl.BlockSpec((1, 1, tf), lambda b, f, be, bv: (be[b], 0, fidx(f, bv, b))),
            pl.BlockSpec((1, 1, tf), lambda b, f, be, bv: (be[b], 0, n_f + fidx(f, bv, b))),
            pl.BlockSpec((1, tf, d), lambda b, f, be, bv: (be[b], fidx(f, bv, b), 0)),
            pl.BlockSpec((1, 1, d), lambda b, f, be, bv: (be[b], 0, 0)),
        ],
        out_specs=pl.BlockSpec((tm, d), lambda b, f, be, bv: (b, 0)),
        scratch_shapes=[
            pltpu.VMEM((2, tm, d), jnp.float32),
            pltpu.VMEM((tm, d), jnp.bfloat16),
            pltpu.SemaphoreType.DMA((2,)),
        ],
    )
    tok3 = tok_rows.reshape(n_blk, 1, tm)
    return pl.pallas_call(
        functools.partial(_expert_kernel, tm=tm, n_f=n_f, n_blk=n_blk),
        grid_spec=grid_spec,
        out_shape=jax.ShapeDtypeStruct((n_blk * tm, d), jnp.float32),
        compiler_params=_cparams(("arbitrary", "arbitrary")),
        name="experts",
    )(blk_expert, blk_valid, tok3, tok3, xn, w_gu, w_gu,
      b_gu.reshape(e, 1, two_f), b_gu.reshape(e, 1, two_f), w_down, b_down.reshape(e, 1, d))


def _combine_kernel(pos_ref, pos_next_ref, h_ref, g_ref, ys_hbm, o_ref, buf_ref, sem, *, tt, nt):
    i = pl.program_id(0)
    slot = i % 2
    n_rows = TOP_K * tt

    @pl.when(i == 0)
    def _():
        _start_row_gather(pos_ref, n_rows, ys_hbm, buf_ref.at[0], sem.at[0])

    @pl.when(i + 1 < nt)
    def _():
        _start_row_gather(pos_next_ref, n_rows, ys_hbm, buf_ref.at[1 - slot], sem.at[1 - slot])

    _wait_row_gather(n_rows, ys_hbm, buf_ref.at[slot], sem.at[slot])
    g = g_ref[...]
    y = h_ref[...]
    for k in range(TOP_K):
        y = y + buf_ref[slot, k * tt:(k + 1) * tt, :] * g[:, k:k + 1]
    o_ref[...] = y


def _combine(h, gates, pos, ys):
    n, d = h.shape
    tt = _pick(n, (384, 256, 128, 64))
    nt = n // tt
    pos_t = pos.reshape(nt, tt, TOP_K).transpose(0, 2, 1).reshape(nt, 1, TOP_K * tt)
    row = lambda i: (i, 0)
    return pl.pallas_call(
        functools.partial(_combine_kernel, tt=tt, nt=nt),
        grid=(nt,),
        in_specs=[
            pl.BlockSpec((1, 1, TOP_K * tt), lambda i: (i, 0, 0), memory_space=pltpu.SMEM),
            pl.BlockSpec((1, 1, TOP_K * tt), lambda i: (jnp.minimum(i + 1, nt - 1), 0, 0), memory_space=pltpu.SMEM),
            pl.BlockSpec((tt, d), row),
            pl.BlockSpec((tt, LANES), row),
            pl.BlockSpec(memory_space=pl.ANY),
        ],
        out_specs=pl.BlockSpec((tt, d), row),
        out_shape=jax.ShapeDtypeStruct((n, d), jnp.float32),
        scratch_shapes=[pltpu.VMEM((2, TOP_K * tt, d), jnp.float32), pltpu.SemaphoreType.DMA((2,))],
        compiler_params=_cparams(("arbitrary",)),
        name="moe_combine",
    )(pos_t, pos_t, h, gates, ys)


def _moe_layout(top_i, n, tm):
    i32 = jnp.int32
    flat_e = top_i.reshape(-1)
    n_pairs = flat_e.shape[0]
    order = jnp.argsort(flat_e).astype(i32)
    inv = jnp.argsort(order).astype(i32)
    onehot = flat_e[:, None] == jnp.arange(N_EXPERTS, dtype=i32)[None, :]
    counts = jnp.sum(onehot, axis=0, dtype=i32)
    nblk_e = (counts + tm - 1) // tm
    blk_end = jnp.cumsum(nblk_e)
    blk_start = blk_end - nblk_e
    cum_excl = jnp.cumsum(counts) - counts
    n_blk = (n_pairs + N_EXPERTS * (tm - 1)) // tm + 1
    blk = jnp.arange(n_blk, dtype=i32)
    blk_valid = (blk < blk_end[-1]).astype(i32)
    blk_expert = jnp.minimum(jnp.sum(blk[:, None] >= blk_end[None, :], axis=1, dtype=i32), N_EXPERTS - 1)
    last_e = jnp.max(jnp.where(nblk_e > 0, jnp.arange(N_EXPERTS, dtype=i32), 0))
    blk_expert = jnp.where(blk_valid > 0, blk_expert, last_e)
    within = blk - blk_start[blk_expert]
    base = jnp.clip(cum_excl[blk_expert] + within * tm, 0, n_pairs)
    n_ok = jnp.clip(counts[blk_expert] - within * tm, 0, tm) * blk_valid
    tok_sorted = jnp.concatenate([order // TOP_K, jnp.zeros((tm,), i32)])
    tok_rows = jax.vmap(lambda s: lax.dynamic_slice(tok_sorted, (s,), (tm,)))(base)
    tok_rows = jnp.where(jnp.arange(tm, dtype=i32)[None, :] < n_ok[:, None], tok_rows, 0)
    delta = blk_start * tm - cum_excl
    pos = inv + jnp.sum(jnp.where(onehot, delta[None, :], 0), axis=1, dtype=i32)
    return blk_expert, blk_valid, tok_rows.reshape(-1), pos.reshape(n, TOP_K)


def _pack_in_proj(w_in, d_model):
    hk = d_model // 2
    hv = d_model
    o = 0
    cuts = {}
    for name, width in (("gq", hk), ("gk", hk), ("gv", hv), ("gr", hv), ("ga", GLA_RANK), ("aq", d_model),
                        ("ak", d_model // GROUP), ("av", d_model // GROUP), ("iq", IDX_HEADS * IDX_DIM),
                        ("ik", IDX_DIM), ("iw", IDX_HEADS), ("za", d_model), ("zb", d_model)):
        cuts[name] = (o, o + width)
        o += width
    assert o == w_in.shape[1]
    col = lambda a, b: w_in[:, cuts[a][0]:cuts[b][1]]
    w1 = col("gq", "gr")
    w2 = col("aq", "av")
    w4 = col("za", "zb")
    k = w_in.shape[0]
    iq = jnp.pad(col("iq", "iq").reshape(k, IDX_HEADS, IDX_DIM), ((0, 0), (0, 0), (0, LANES - IDX_DIM)))
    pad_to = lambda a: jnp.pad(a, ((0, 0), (0, LANES - a.shape[1])))
    w3 = jnp.concatenate([iq.reshape(k, IDX_HEADS * LANES), pad_to(col("ik", "ik")), pad_to(col("ga", "ga")),
                          pad_to(col("iw", "iw"))], axis=1)
    bf = lambda a: a.astype(jnp.bfloat16)
    return bf(w1), bf(w2), bf(w3), bf(w4)


def kernel(x_prompt, x_sample, cache_k, cache_v, cache_kidx, state_gla, norm1_g, w_in, w_alpha2, b_alpha, gla_norm_g, q_norm_g, k_norm_g, w_proj_a, w_proj_b, w_out, norm2_g, w_router, b_router, w_gu, b_gu, w_down, b_down):
    B, T, D = x_prompt.shape
    DB, DS, _ = x_sample.shape
    depth = norm1_g.shape[0]
    NP, NS = B * T, DB * DS
    N = NP + NS
    past = cache_k.shape[2]
    hd = D // ATT_HEADS
    dk, dv = D // (2 * GLA_HEADS), D // GLA_HEADS
    kvw = ATT_KV_HEADS * hd
    bf16 = jnp.bfloat16

    pos = jnp.concatenate([jnp.tile(jnp.arange(T, dtype=jnp.int32), B),
                           jnp.tile(past + jnp.arange(DS, dtype=jnp.int32), DB)])
    x = jnp.concatenate([x_prompt.reshape(NP, D), x_sample.reshape(NS, D)], axis=0)
    outs = {k: [] for k in ("kp", "vp", "kip", "sp", "ks", "vs", "kis", "ss")}

    for l in range(depth):
        w1, w2, w3, w4 = _pack_in_proj(w_in[l], D)
        xn = _rmsnorm(x, norm1_g[l], bf16)
        z1 = _matmul(xn, w1, name="in_proj_gla")
        z2 = _matmul(xn, w2, name="in_proj_attn")
        z3 = _matmul(xn, w3, name="in_proj_idx")
        z4 = _matmul(xn, w4, name="in_proj_gate")

        w2p = jnp.pad(w_alpha2[l], ((0, LANES - GLA_RANK), (0, 0)))
        w2t = w2p.T
        oa_p, S_p = _gla(z1, z3, 0, B, T, w2p, w2t, b_alpha[l], gla_norm_g[l], None, dk, dv)
        oa_s, S_s = _gla(z1, z3, NP, DB, DS, w2p, w2t, b_alpha[l], gla_norm_g[l], state_gla[l], dk, dv)
        o_a = jnp.concatenate([oa_p, oa_s], axis=0)

        q_hm, k_f, k_b, v_b, qc_hm, ki_f, kc, w_pad = _prep(z2, z3, q_norm_g[l], k_norm_g[l], pos, hd)
        v_f = z2[:, ATT_HEADS * hd + kvw:]
        topk_p = min(TOPK_MAX, T // 4)
        tk_p = 1024 if T % 1024 == 0 else (256 if T % 256 == 0 else T)
        ob_p = _sparse_attention(q_hm[:, :NP], qc_hm[:, :NP], w_pad[:NP], kc[:NP].reshape(B, T, -1),
                                 k_b[:NP].reshape(B, T, kvw), v_b[:NP].reshape(B, T, kvw),
                                 B, T, T, 0, topk_p, 128, tk_p)
        lk_s = past + DS
        lk_pad = -(-lk_s // LANES) * LANES
        padk = lambda a: jnp.pad(a, ((0, 0), (0, lk_pad - lk_s), (0, 0)))
        ck_hi = cache_kidx[l].astype(bf16)
        ck_lo = (cache_kidx[l] - ck_hi.astype(jnp.float32)).astype(bf16)
        kc_s = padk(jnp.concatenate([jnp.concatenate([ck_hi, ck_hi, ck_lo, ck_lo], axis=-1),
                                     kc[NP:].reshape(DB, DS, -1)], axis=1))
        kb_s = padk(jnp.concatenate([cache_k[l].reshape(DB, past, kvw).astype(bf16),
                                     k_b[NP:].reshape(DB, DS, kvw)], axis=1))
        vb_s = padk(jnp.concatenate([cache_v[l].reshape(DB, past, kvw).astype(bf16),
                                     v_b[NP:].reshape(DB, DS, kvw)], axis=1))
        topk_s = min(TOPK_MAX, lk_s // 4)
        ob_s = _sparse_attention(q_hm[:, NP:], qc_hm[:, NP:], w_pad[NP:], kc_s, kb_s, vb_s,
                                 DB, DS, lk_s, past, topk_s, DS, lk_pad)
        o_b = jnp.concatenate([ob_p, ob_s], axis=0)

        mixed = _merge(o_a, o_b, w_proj_a[l].astype(bf16), w_proj_b[l].astype(bf16), z4)
        h = _resid_matmul(x, mixed, w_out[l].astype(bf16))

        xn2, top_i, gates = _router(h, norm2_g[l], w_router[l], b_router[l])
        tm_e = 512 if N >= 8192 else 64
        blk_e, blk_v, tok_rows, pos_rows = _moe_layout(top_i[:, :TOP_K], N, tm_e)
        ys = _experts(xn2, blk_e, blk_v, tok_rows, w_gu[l].astype(bf16), b_gu[l], w_down[l].astype(bf16),
                      b_down[l], tm_e)
        x = _combine(h, gates, pos_rows, ys)

        outs["kp"].append(k_f[:NP].reshape(B, T, ATT_KV_HEADS, hd))
        outs["vp"].append(v_f[:NP].reshape(B, T, ATT_KV_HEADS, hd))
        outs["kip"].append(ki_f[:NP].reshape(B, T, IDX_DIM))
        outs["sp"].append(S_p)
        outs["ks"].append(k_f[NP:].reshape(DB, DS, ATT_KV_HEADS, hd))
        outs["vs"].append(v_f[NP:].reshape(DB, DS, ATT_KV_HEADS, hd))
        outs["kis"].append(ki_f[NP:].reshape(DB, DS, IDX_DIM))
        outs["ss"].append(S_s)

    st = lambda k: jnp.stack(outs[k], axis=0)
    return (x[:NP].reshape(B, T, D), x[NP:].reshape(DB, DS, D), st("kp"), st("vp"), st("kip"), st("sp"),
            st("ks"), st("vs"), st("kis"), st("ss"))
```

```python
import functools
import math

import jax
import jax.numpy as jnp
from jax import lax
from jax.experimental import pallas as pl
from jax.experimental.pallas import tpu as pltpu

CHUNK = 64
GLA_HEADS = 4
GLA_RANK = 16
GLA_TAU = 16.0
ATT_HEADS = 16
ATT_KV_HEADS = 4
GROUP = ATT_HEADS // ATT_KV_HEADS
ROPE_THETA = 500000.0
IDX_HEADS = 8
IDX_DIM = 64
TOPK_MAX = 256
N_EXPERTS = 32
TOP_K = 4
SWIGLU_LIMIT = 7.0
SWIGLU_ALPHA = 1.702
EPS = 1e-6

LANES = 128
VMEM_LIMIT = 56 * 1024 * 1024
INT_MIN = -(2 ** 31)
ATTN_ROW_CHUNK = 16
SCORE_COLS = 256
LOG2E = 1.4426950408889634
NEG_INIT = -1e30
MASK_BIG = 2.0 ** 100
M_INIT = -(2.0 ** 120)

_HIGHEST = lax.Precision.HIGHEST
_NT = (((1,), (1,)), ((), ()))
_TN = (((0,), (0,)), ((), ()))


def _pick(n, cands):
    for c in cands:
        if n % c == 0:
            return c
    raise ValueError(f"no tile in {cands} divides {n}")


def _cparams(sem):
    return pltpu.CompilerParams(dimension_semantics=sem, vmem_limit_bytes=VMEM_LIMIT)


def _rmsnorm_kernel(x_ref, g_ref, o_ref):
    x = x_ref[...]
    y = x * lax.rsqrt(jnp.mean(x * x, axis=-1, keepdims=True) + EPS) * g_ref[...]
    o_ref[...] = y.astype(o_ref.dtype)


def _rmsnorm(x, g, out_dtype):
    n, d = x.shape
    tm = _pick(n, (768, 512, 256, 128, 64))
    return pl.pallas_call(
        _rmsnorm_kernel,
        grid=(n // tm,),
        in_specs=[pl.BlockSpec((tm, d), lambda i: (i, 0)), pl.BlockSpec((1, d), lambda i: (0, 0))],
        out_specs=pl.BlockSpec((tm, d), lambda i: (i, 0)),
        out_shape=jax.ShapeDtypeStruct((n, d), out_dtype),
        compiler_params=_cparams(("parallel",)),
        name="rmsnorm",
    )(x, g.reshape(1, d))


def _mm_kernel(a_ref, w_ref, o_ref):
    o_ref[...] = jnp.dot(a_ref[...], w_ref[...], preferred_element_type=jnp.float32).astype(o_ref.dtype)


def _matmul(a, w, out_dtype=jnp.float32, name="matmul"):
    m, k = a.shape
    _, n = w.shape
    tm = _pick(m, (768, 512, 256, 128, 64))
    tn = 1024 if n % 1024 == 0 else (512 if n % 512 == 0 else n)
    return pl.pallas_call(
        _mm_kernel,
        grid=(m // tm, n // tn),
        in_specs=[pl.BlockSpec((tm, k), lambda i, j: (i, 0)), pl.BlockSpec((k, tn), lambda i, j: (0, j))],
        out_specs=pl.BlockSpec((tm, tn), lambda i, j: (i, j)),
        out_shape=jax.ShapeDtypeStruct((m, n), out_dtype),
        compiler_params=_cparams(("parallel", "arbitrary")),
        name=name,
    )(a, w)


def _merge_kernel(oa_ref, ob_ref, wa_ref, wb_ref, za_ref, zb_ref, o_ref):
    pa = jnp.dot(oa_ref[...], wa_ref[...], preferred_element_type=jnp.float32)
    pb = jnp.dot(ob_ref[...], wb_ref[...], preferred_element_type=jnp.float32)
    o_ref[...] = (jax.nn.sigmoid(za_ref[...]) * pa + jax.nn.sigmoid(zb_ref[...]) * pb).astype(o_ref.dtype)


def _merge(o_a, o_b, wa, wb, z4):
    m, k = o_a.shape
    n = wa.shape[1]
    tm = _pick(m, (768, 512, 256, 128, 64))
    tn = 512
    nj = n // tn
    return pl.pallas_call(
        _merge_kernel,
        grid=(m // tm, nj),
        in_specs=[
            pl.BlockSpec((tm, k), lambda i, j: (i, 0)),
            pl.BlockSpec((tm, k), lambda i, j: (i, 0)),
            pl.BlockSpec((k, tn), lambda i, j: (0, j)),
            pl.BlockSpec((k, tn), lambda i, j: (0, j)),
            pl.BlockSpec((tm, tn), lambda i, j: (i, j)),
            pl.BlockSpec((tm, tn), lambda i, j: (i, j + nj)),
        ],
        out_specs=pl.BlockSpec((tm, tn), lambda i, j: (i, j)),
        out_shape=jax.ShapeDtypeStruct((m, n), jnp.bfloat16),
        compiler_params=_cparams(("parallel", "arbitrary")),
        name="merge",
    )(o_a, o_b, wa, wb, z4, z4)


def _resid_mm_kernel(x_ref, a_ref, w_ref, o_ref):
    o_ref[...] = x_ref[...] + jnp.dot(a_ref[...], w_ref[...], preferred_element_type=jnp.float32)


def _resid_matmul(x, a, w):
    m, k = a.shape
    n = w.shape[1]
    tm = _pick(m, (768, 512, 256, 128, 64))
    tn = 1024 if n % 1024 == 0 else n
    return pl.pallas_call(
        _resid_mm_kernel,
        grid=(m // tm, n // tn),
        in_specs=[
            pl.BlockSpec((tm, tn), lambda i, j: (i, j)),
            pl.BlockSpec((tm, k), lambda i, j: (i, 0)),
            pl.BlockSpec((k, tn), lambda i, j: (0, j)),
        ],
        out_specs=pl.BlockSpec((tm, tn), lambda i, j: (i, j)),
        out_shape=jax.ShapeDtypeStruct((m, n), jnp.float32),
        compiler_params=_cparams(("parallel", "arbitrary")),
        name="out_proj",
    )(x, a, w)


def _log_sigmoid(x):
    return jnp.minimum(x, 0.0) - jnp.log1p(jnp.exp(-jnp.abs(x)))


def _gla_kernel(*refs, c, dk, dv, has_init):
    if has_init:
        (z1_ref, ga_ref, w2_ref, w2t_ref, b_ref, bcol_ref, gn_ref, s0_ref, o_ref, sout_ref, s_scr) = refs
    else:
        (z1_ref, ga_ref, w2_ref, w2t_ref, b_ref, bcol_ref, gn_ref, o_ref, sout_ref, s_scr) = refs
        s0_ref = None
    ci = pl.program_id(1)
    hk = GLA_HEADS * dk

    @pl.when(ci == 0)
    def _():
        if has_init:
            s_scr[...] = s0_ref[0]
        else:
            s_scr[...] = jnp.zeros_like(s_scr)

    ga = ga_ref[...]
    g = _log_sigmoid(jnp.dot(ga, w2_ref[...], precision=_HIGHEST,
                             preferred_element_type=jnp.float32) + b_ref[...]) / GLA_TAU
    row = lax.broadcasted_iota(jnp.int32, (c, c), 0)
    col = lax.broadcasted_iota(jnp.int32, (c, c), 1)
    causal = col <= row
    b = jnp.dot(causal.astype(jnp.float32), g, precision=_HIGHEST, preferred_element_type=jnp.float32)
    gt = _log_sigmoid(lax.dot_general(w2t_ref[...], ga, _NT, precision=_HIGHEST,
                                      preferred_element_type=jnp.float32) + bcol_ref[...]) / GLA_TAU
    bend_col = jnp.sum(gt, axis=1, keepdims=True)
    scale = dk ** -0.5
    for h in range(GLA_HEADS):
        bh = b[:, h * dk:(h + 1) * dk]
        q = z1_ref[:, h * dk:(h + 1) * dk] * scale
        k = z1_ref[:, hk + h * dk: hk + (h + 1) * dk]
        v = z1_ref[:, 2 * hk + h * dv: 2 * hk + (h + 1) * dv].astype(jnp.bfloat16)
        r = z1_ref[:, 2 * hk + GLA_HEADS * dv + h * dv: 2 * hk + GLA_HEADS * dv + (h + 1) * dv]
        qe = (q * jnp.exp(bh)).astype(jnp.bfloat16)
        ke = (k * jnp.exp(-bh)).astype(jnp.bfloat16)
        a = lax.dot_general(qe, ke, _NT, preferred_element_type=jnp.float32)
        a = jnp.where(causal, a, 0.0).astype(jnp.bfloat16)
        s = s_scr[h]
        o = (jnp.dot(a, v, preferred_element_type=jnp.float32)
             + jnp.dot(qe, s.astype(jnp.bfloat16), preferred_element_type=jnp.float32))
        b_end = bh[c - 1:c, :]
        kd = (k * jnp.exp(b_end - bh)).astype(jnp.bfloat16)
        s_scr[h] = (jnp.exp(bend_col[h * dk:(h + 1) * dk, :]) * s
                    + lax.dot_general(kd, v, _TN, preferred_element_type=jnp.float32))
        on = o * lax.rsqrt(jnp.mean(o * o, axis=-1, keepdims=True) + EPS) * gn_ref[...]
        o_ref[:, h * dv:(h + 1) * dv] = (on * (r * jax.nn.sigmoid(r))).astype(o_ref.dtype)

    @pl.when(ci == pl.num_programs(1) - 1)
    def _():
        sout_ref[0] = s_scr[...]


def _gla(z1, z3, row0, n_seq, seq_len, w2p, w2t, b_alpha, gn, s0, dk, dv):
    c = CHUNK if seq_len % CHUNK == 0 else seq_len
    nc = seq_len // c
    hk = GLA_HEADS * dk
    blk0 = row0 // c
    ga_blk = (IDX_HEADS + 1)
    has_init = s0 is not None
    rows = n_seq * seq_len
    in_specs = [
        pl.BlockSpec((c, z1.shape[1]), lambda s, i: (blk0 + s * nc + i, 0)),
        pl.BlockSpec((c, LANES), lambda s, i: (blk0 + s * nc + i, ga_blk)),
        pl.BlockSpec((LANES, hk), lambda s, i: (0, 0)),
        pl.BlockSpec((hk, LANES), lambda s, i: (0, 0)),
        pl.BlockSpec((1, hk), lambda s, i: (0, 0)),
        pl.BlockSpec((hk, 1), lambda s, i: (0, 0)),
        pl.BlockSpec((1, dv), lambda s, i: (0, 0)),
    ]
    args = [z1, z3, w2p, w2t, b_alpha.reshape(1, hk), b_alpha.reshape(hk, 1), gn.reshape(1, dv)]
    if has_init:
        in_specs.append(pl.BlockSpec((1, GLA_HEADS, dk, dv), lambda s, i: (s, 0, 0, 0)))
        args.append(s0)
    return pl.pallas_call(
        functools.partial(_gla_kernel, c=c, dk=dk, dv=dv, has_init=has_init),
        grid=(n_seq, nc),
        in_specs=in_specs,
        out_specs=[
            pl.BlockSpec((c, GLA_HEADS * dv), lambda s, i: (s * nc + i, 0)),
            pl.BlockSpec((1, GLA_HEADS, dk, dv), lambda s, i: (s, 0, 0, 0)),
        ],
        out_shape=[
            jax.ShapeDtypeStruct((rows, GLA_HEADS * dv), jnp.bfloat16),
            jax.ShapeDtypeStruct((n_seq, GLA_HEADS, dk, dv), jnp.float32),
        ],
        scratch_shapes=[pltpu.VMEM((GLA_HEADS, dk, dv), jnp.float32)],
        compiler_params=_cparams(("parallel", "arbitrary")),
        name="gla_init" if has_init else "gla",
    )(*args)


def _rope(x, cos, sin, half):
    lane = lax.broadcasted_iota(jnp.int32, x.shape, 1)
    partner = jnp.where(lane < half, pltpu.roll(x, LANES - half, 1), pltpu.roll(x, half, 1))
    return x * cos + partner * sin


def _split_hi_lo(x):
    hi = x.astype(jnp.bfloat16).astype(jnp.float32)
    lo = (x - hi).astype(jnp.bfloat16).astype(jnp.float32)
    return hi, lo


def _prep_kernel(z2_ref, z3_ref, qg_ref, kg_ref, ca_ref, sa_ref, ci_ref, si_ref,
                 q_ref, kf_ref, kb_ref, vb_ref, qc_ref, kif_ref, kc_ref, w_ref, *, hd):
    ca, sa, ci, si = ca_ref[...], sa_ref[...], ci_ref[...], si_ref[...]
    qscale = hd ** -0.5 * LOG2E

    def headnorm(x, g):
        return x * lax.rsqrt(jnp.mean(x * x, axis=-1, keepdims=True) + EPS) * g

    for h in range(ATT_HEADS):
        x = headnorm(z2_ref[:, h * hd:(h + 1) * hd], qg_ref[...])
        q_ref[h] = (_rope(x, ca, sa, hd // 8) * qscale).astype(q_ref.dtype)
    k0 = ATT_HEADS * hd
    for h in range(ATT_KV_HEADS):
        x = headnorm(z2_ref[:, k0 + h * hd:k0 + (h + 1) * hd], kg_ref[...])
        y = _rope(x, ca, sa, hd // 8)
        kf_ref[:, h * hd:(h + 1) * hd] = y
        kb_ref[:, h * hd:(h + 1) * hd] = y.astype(kb_ref.dtype)
    v0 = k0 + ATT_KV_HEADS * hd
    vb_ref[...] = z2_ref[:, v0:v0 + ATT_KV_HEADS * hd].astype(vb_ref.dtype)
    for h in range(IDX_HEADS):
        y = _rope(z3_ref[:, h * LANES:(h + 1) * LANES], ci, si, IDX_DIM // 8)
        hi, lo = _split_hi_lo(y)
        t = (hi + pltpu.roll(lo, IDX_DIM, 1)).astype(qc_ref.dtype)
        qc_ref[h] = jnp.concatenate([t, t], axis=1)
    y = _rope(z3_ref[:, IDX_HEADS * LANES:(IDX_HEADS + 1) * LANES], ci, si, IDX_DIM // 8)
    kif_ref[...] = y[:, :IDX_DIM]
    hi, lo = _split_hi_lo(y)
    kc_ref[...] = jnp.concatenate([hi + pltpu.roll(hi, IDX_DIM, 1), lo + pltpu.roll(lo, IDX_DIM, 1)],
                                  axis=1).astype(kc_ref.dtype)
    w_ref[...] = z3_ref[:, (IDX_HEADS + 2) * LANES:(IDX_HEADS + 3) * LANES] * ((IDX_HEADS * IDX_DIM) ** -0.5)


def _rope_tables(pos, rot):
    half = rot // 2
    inv = ROPE_THETA ** (-2.0 * jnp.arange(half, dtype=jnp.float32) / rot)
    ang = pos.astype(jnp.float32)[:, None] * inv[None, :]
    cos, sin = jnp.cos(ang), jnp.sin(ang)
    n = pos.shape[0]
    ones = jnp.ones((n, LANES - rot), jnp.float32)
    zeros = jnp.zeros((n, LANES - rot), jnp.float32)
    return (jnp.concatenate([cos, cos, ones], axis=1), jnp.concatenate([-sin, sin, zeros], axis=1))


def _prep(z2, z3, q_norm_g, k_norm_g, pos, hd):
    n = z2.shape[0]
    tr = _pick(n, (256, 128, 64))
    ca, sa = _rope_tables(pos, hd // 4)
    ci, si = _rope_tables(pos, IDX_DIM // 4)
    row = lambda i: (i, 0)
    tab = pl.BlockSpec((tr, LANES), row)
    kvw = ATT_KV_HEADS * hd
    return pl.pallas_call(
        functools.partial(_prep_kernel, hd=hd),
        grid=(n // tr,),
        in_specs=[
            pl.BlockSpec((tr, z2.shape[1]), row),
            pl.BlockSpec((tr, z3.shape[1]), row),
            pl.BlockSpec((1, hd), lambda i: (0, 0)),
            pl.BlockSpec((1, hd), lambda i: (0, 0)),
            tab, tab, tab, tab,
        ],
        out_specs=[
            pl.BlockSpec((ATT_HEADS, tr, hd), lambda i: (0, i, 0)),
            pl.BlockSpec((tr, kvw), row),
            pl.BlockSpec((tr, kvw), row),
            pl.BlockSpec((tr, kvw), row),
            pl.BlockSpec((IDX_HEADS, tr, 4 * IDX_DIM), lambda i: (0, i, 0)),
            pl.BlockSpec((tr, IDX_DIM), row),
            pl.BlockSpec((tr, 4 * IDX_DIM), row),
            pl.BlockSpec((tr, LANES), row),
        ],
        out_shape=[
            jax.ShapeDtypeStruct((ATT_HEADS, n, hd), jnp.bfloat16),
            jax.ShapeDtypeStruct((n, kvw), jnp.float32),
            jax.ShapeDtypeStruct((n, kvw), jnp.bfloat16),
            jax.ShapeDtypeStruct((n, kvw), jnp.bfloat16),
            jax.ShapeDtypeStruct((IDX_HEADS, n, 4 * IDX_DIM), jnp.bfloat16),
            jax.ShapeDtypeStruct((n, IDX_DIM), jnp.float32),
            jax.ShapeDtypeStruct((n, 4 * IDX_DIM), jnp.bfloat16),
            jax.ShapeDtypeStruct((n, LANES), jnp.float32),
        ],
        compiler_params=_cparams(("parallel",)),
        name="attn_prep",
    )(z2, z3, q_norm_g.reshape(1, hd), k_norm_g.reshape(1, hd), ca, sa, ci, si)


def _needed_tiles(i, tq, tk, lk_true, pos_off):
    last = i * tq + pos_off + tq - 1
    max_limit = jnp.minimum((last // CHUNK + 1) * CHUNK, lk_true)
    return (max_limit + tk - 1) // tk


def _attn_kernel(qi_ref, kt_ref, q_ref, qc_ref, w_ref, kc_ref, k_ref, v_ref, o_ref,
                 keys_ref, hi_ref, lo_ref, thr_ref, s_ref, p_ref, alpha_ref, m_ref, acc_ref,
                 *, tq, tk, lk_true, pos_off, topk, nbits, hd):
    step = pl.program_id(1)
    i = qi_ref[step]
    kt = kt_ref[step]
    n_need = _needed_tiles(i, tq, tk, lk_true, pos_off)
    n_sub = tk // LANES
    rows = GROUP * tq
    rc = min(ATTN_ROW_CHUNK, tq)

    def tile(j):
        return pl.ds(pl.multiple_of(j * tk, tk), tk)

    def lanes(j, c):
        return pl.ds(pl.multiple_of(j * tk + c * LANES, LANES), LANES)

    @pl.when(kt == 0)
    def _select():
        rowi = lax.broadcasted_iota(jnp.int32, (tq, 1), 0)
        limit = jnp.minimum(((i * tq + pos_off + rowi) // CHUNK + 1) * CHUNK, lk_true)
        k_eff = jnp.minimum(topk, limit).astype(jnp.float32)
        w = w_ref[...]
        qc_all = qc_ref[...].reshape(IDX_HEADS * tq, 4 * IDX_DIM)
        cw = SCORE_COLS if tk % SCORE_COLS == 0 else tk

        def score_body(j, carry):
            for cs in range(tk // cw):
                cols = pl.ds(pl.multiple_of(j * tk + cs * cw, cw), cw)
                s_all = lax.dot_general(qc_all, kc_ref[0, cols, :], _NT, preferred_element_type=jnp.float32)
                acc = jnp.zeros((tq, cw), jnp.float32)
                for h in range(IDX_HEADS):
                    acc = acc + jnp.maximum(s_all[h * tq:(h + 1) * tq], 0.0) * w[:, h:h + 1]
                bits = pltpu.bitcast(acc, jnp.int32)
                sgn = bits >> 31
                key = (bits ^ (sgn & 0x7FFFFFFF)) - sgn
                col = j * tk + cs * cw + lax.broadcasted_iota(jnp.int32, (tq, cw), 1)
                key = jnp.where(col < limit, key, INT_MIN)
                keys_ref[:, cols] = key
                hi_ref[:, cols] = (key >> 16).astype(jnp.int16)
            return carry

        lax.fori_loop(0, n_need, score_body, 0)

        def count(pred):
            def body(j, acc):
                for c in range(n_sub):
                    kk = keys_ref[:, lanes(j, c)]
                    col = j * tk + c * LANES + lax.broadcasted_iota(jnp.int32, (tq, LANES), 1)
                    acc = acc + jnp.where(pred(kk, col), 1.0, 0.0)
                return acc
            acc = lax.fori_loop(0, n_need, body, jnp.zeros((tq, LANES), jnp.float32))
            return jnp.sum(acc, axis=1, keepdims=True)

        def count16(ref, pred):
            one = jnp.ones((tq, LANES), jnp.int16)
            zero = jnp.zeros((tq, LANES), jnp.int16)

            def body(j, acc):
                for c in range(n_sub):
                    acc = acc + jnp.where(pred(ref[:, lanes(j, c)]), one, zero)
                return acc
            acc = lax.fori_loop(0, n_need, body, zero)
            return jnp.sum(acc.astype(jnp.float32), axis=1, keepdims=True)

        def as16(t):
            return jnp.broadcast_to(t, (tq, LANES)).astype(jnp.int16)

        def search16(ref, target):
            def bit_body(b, cand):
                t = cand + lax.shift_left(jnp.int32(1), 15 - b)
                t16 = as16(t)
                c = count16(ref, lambda kk: kk >= t16)
                return jnp.where(c >= target, t, cand)
            return lax.fori_loop(0, 16, bit_body, jnp.full((tq, 1), -32768, jnp.int32))

        thr_hi = search16(hi_ref, k_eff)
        thr_hi16 = as16(thr_hi)
        k_lo = k_eff - count16(hi_ref, lambda kk: kk > thr_hi16)

        def lo_body(j, carry):
            key = keys_ref[:, tile(j)]
            lo = jnp.where((key >> 16) == thr_hi, (key & 0xFFFF) - 32768, -32768)
            lo_ref[:, tile(j)] = lo.astype(jnp.int16)
            return carry

        lax.fori_loop(0, n_need, lo_body, 0)
        thr_lo = search16(lo_ref, k_lo)
        thr = thr_hi * 65536 + (thr_lo + 32768)
        thr_ref[...] = thr
        c_gt = count(lambda kk, col: kk > thr)
        c_ge = count(lambda kk, col: kk >= thr)
        need = k_eff - c_gt
        excess = jnp.max(c_ge - k_eff) > 0.5

        @pl.when(excess)
        def _ties():
            def idx_body(b, cut):
                t = cut + lax.shift_left(jnp.int32(1), nbits - 1 - b)
                c = count(lambda kk, col: jnp.where(kk == thr, col, t) < t)
                return jnp.where(c <= need, t, cut)

            cut = lax.fori_loop(0, nbits, idx_body, jnp.zeros((tq, 1), jnp.int32))

            def fix_body(j, carry):
                kk = keys_ref[:, tile(j)]
                col = j * tk + lax.broadcasted_iota(jnp.int32, (tq, tk), 1)
                drop = jnp.where(kk == thr, col, -1) >= cut
                keys_ref[:, tile(j)] = jnp.where(drop, INT_MIN, kk)
                return carry

            lax.fori_loop(0, n_need, fix_body, 0)

        m_ref[...] = jnp.full(m_ref.shape, M_INIT, jnp.float32)
        acc_ref[...] = jnp.zeros(acc_ref.shape, jnp.float32)

    def _attend():
        sel = keys_ref[:, tile(kt)] >= thr_ref[...]
        mask = jnp.where(sel, 0.0, -MASK_BIG)
        if tq < LANES:
            mask = jnp.concatenate([mask, jnp.zeros((LANES - tq, tk), jnp.float32)], axis=0)
        mask_t = mask.T.astype(k_ref.dtype)
        rowi = lax.broadcasted_iota(jnp.int32, (tq, LANES), 0)
        lanei = lax.broadcasted_iota(jnp.int32, (tq, LANES), 1)
        eye = jnp.where(rowi == lanei, 1.0, 0.0).astype(q_ref.dtype)
        eye4 = jnp.concatenate([eye] * GROUP, axis=0)

        def qk(kvh):
            q4 = jnp.concatenate([q_ref[kvh * GROUP:(kvh + 1) * GROUP].reshape(rows, hd), eye4], axis=1)
            kb = jnp.concatenate([k_ref[0, :, kvh * hd:(kvh + 1) * hd], mask_t], axis=1)
            s_ref[kvh % 2] = lax.dot_general(q4, kb, _NT, preferred_element_type=jnp.float32)

        qk(0)
        for kvh in range(ATT_KV_HEADS):
            if kvh + 1 < ATT_KV_HEADS:
                qk(kvh + 1)
            sb = kvh % 2
            for c in range(rows // rc):
                r0 = c * rc
                t = s_ref[sb, r0:r0 + rc, :]
                m_prev = m_ref[kvh, r0:r0 + rc, :]
                m_new = jnp.maximum(m_prev, jnp.max(t, axis=1, keepdims=True))
                p_ref[sb, r0:r0 + rc, :] = jnp.exp2((t - m_new).astype(p_ref.dtype))
                alpha_ref[sb, r0:r0 + rc, :] = jnp.exp2(m_prev - m_new)
                m_ref[kvh, r0:r0 + rc, :] = m_new
            vb = jnp.concatenate([v_ref[0, :, kvh * hd:(kvh + 1) * hd], jnp.ones((tk, LANES), v_ref.dtype)], axis=1)
            acc_ref[kvh] = alpha_ref[sb] * acc_ref[kvh] + jnp.dot(p_ref[sb], vb, preferred_element_type=jnp.float32)

    _attend()

    @pl.when(kt == n_need - 1)
    def _finish():
        for kvh in range(ATT_KV_HEADS):
            o = acc_ref[kvh, :, 0:hd] / acc_ref[kvh, :, hd:hd + 1]
            for g in range(GROUP):
                h = kvh * GROUP + g
                o_ref[:, h * hd:(h + 1) * hd] = o[g * tq:(g + 1) * tq].astype(o_ref.dtype)


def _sparse_attention(q_hm, qcat_hm, w_pad, kcat, kb, vb, n_grp, tq_total, lk_true, pos_off, topk, tq, tk):
    hd = q_hm.shape[-1]
    lk_pad = kb.shape[1]
    nq = tq_total // tq
    nbits = max(1, math.ceil(math.log2(lk_pad + 1)))
    pairs = [(i, kt) for i in range(nq)
             for kt in range(-(-min(((i * tq + pos_off + tq - 1) // CHUNK + 1) * CHUNK, lk_true) // tk))]
    qi_of = jnp.asarray([p[0] for p in pairs], jnp.int32)
    kt_of = jnp.asarray([p[1] for p in pairs], jnp.int32)
    qblk = lambda g, s, qi, kt: (0, g * nq + qi[s], 0)
    qrow = lambda g, s, qi, kt: (g * nq + qi[s], 0)
    kv_map = lambda g, s, qi, kt: (g, kt[s], 0)
    kern = functools.partial(_attn_kernel, tq=tq, tk=tk, lk_true=lk_true, pos_off=pos_off,
                             topk=topk, nbits=nbits, hd=hd)
    grid_spec = pltpu.PrefetchScalarGridSpec(
        num_scalar_prefetch=2,
        grid=(n_grp, len(pairs)),
        in_specs=[
            pl.BlockSpec((ATT_HEADS, tq, hd), qblk),
            pl.BlockSpec((IDX_HEADS, tq, 4 * IDX_DIM), qblk),
            pl.BlockSpec((tq, LANES), qrow),
            pl.BlockSpec((1, lk_pad, 4 * IDX_DIM), lambda g, s, qi, kt: (g, 0, 0)),
            pl.BlockSpec((1, tk, ATT_KV_HEADS * hd), kv_map),
            pl.BlockSpec((1, tk, ATT_KV_HEADS * hd), kv_map),
        ],
        out_specs=pl.BlockSpec((tq, ATT_HEADS * hd), qrow),
        scratch_shapes=[
            pltpu.VMEM((tq, lk_pad), jnp.int32),
            pltpu.VMEM((tq, lk_pad), jnp.int16),
            pltpu.VMEM((tq, lk_pad), jnp.int16),
            pltpu.VMEM((tq, 1), jnp.int32),
            pltpu.VMEM((2, GROUP * tq, tk), jnp.float32),
            pltpu.VMEM((2, GROUP * tq, tk), jnp.bfloat16),
            pltpu.VMEM((2, GROUP * tq, 1), jnp.float32),
            pltpu.VMEM((ATT_KV_HEADS, GROUP * tq, 1), jnp.float32),
            pltpu.VMEM((ATT_KV_HEADS, GROUP * tq, hd + LANES), jnp.float32),
        ],
    )
    return pl.pallas_call(
        kern,
        grid_spec=grid_spec,
        out_shape=jax.ShapeDtypeStruct((n_grp * tq_total, ATT_HEADS * hd), jnp.bfloat16),
        compiler_params=_cparams(("parallel", "arbitrary")),
        name="sparse_attn",
    )(qi_of, kt_of, q_hm, qcat_hm, w_pad, kcat, kb, vb)


def _router_kernel(h_ref, g_ref, wr_ref, br_ref, xn_ref, ti_ref, tg_ref):
    h = h_ref[...]
    xn = h * lax.rsqrt(jnp.mean(h * h, axis=-1, keepdims=True) + EPS) * g_ref[...]
    xn_ref[...] = xn
    logits = jnp.dot(xn, wr_ref[...], precision=_HIGHEST, preferred_element_type=jnp.float32) + br_ref[...]
    lane = lax.broadcasted_iota(jnp.int32, logits.shape, 1)
    cur = logits
    vals, idxs = [], []
    for _ in range(TOP_K):
        m = jnp.max(cur, axis=1, keepdims=True)
        idx = jnp.min(jnp.where(cur == m, lane, LANES), axis=1, keepdims=True)
        vals.append(m)
        idxs.append(idx)
        cur = jnp.where(lane == idx, -jnp.inf, cur)
    es = [jnp.exp(v - vals[0]) for v in vals]
    denom = es[0] + es[1] + es[2] + es[3]
    ti = jnp.zeros(logits.shape, jnp.int32)
    tg = jnp.zeros(logits.shape, jnp.float32)
    for k in range(TOP_K):
        ti = jnp.where(lane == k, idxs[k], ti)
        tg = jnp.where(lane == k, es[k] / denom, tg)
    ti_ref[...] = ti
    tg_ref[...] = tg


def _router(h, g, w_router, b_router):
    n, d = h.shape
    tm = _pick(n, (768, 512, 256, 128, 64))
    wr = jnp.pad(w_router, ((0, 0), (0, LANES - N_EXPERTS)))
    br = jnp.pad(b_router, (0, LANES - N_EXPERTS), constant_values=NEG_INIT).reshape(1, LANES)
    row = lambda i: (i, 0)
    return pl.pallas_call(
        _router_kernel,
        grid=(n // tm,),
        in_specs=[
            pl.BlockSpec((tm, d), row),
            pl.BlockSpec((1, d), lambda i: (0, 0)),
            pl.BlockSpec((d, LANES), lambda i: (0, 0)),
            pl.BlockSpec((1, LANES), lambda i: (0, 0)),
        ],
        out_specs=[pl.BlockSpec((tm, d), row), pl.BlockSpec((tm, LANES), row), pl.BlockSpec((tm, LANES), row)],
        out_shape=[
            jax.ShapeDtypeStruct((n, d), jnp.float32),
            jax.ShapeDtypeStruct((n, LANES), jnp.int32),
            jax.ShapeDtypeStruct((n, LANES), jnp.float32),
        ],
        compiler_params=_cparams(("parallel",)),
        name="router",
    )(h, g.reshape(1, d), wr, br)


def _start_row_gather(idx_ref, n_rows, src_hbm, dst_ref, sem):
    def start(r, c):
        pltpu.make_async_copy(src_hbm.at[pl.ds(idx_ref[0, 0, r], 1), :], dst_ref.at[pl.ds(r, 1), :], sem).start()
        return c

    lax.fori_loop(0, n_rows, start, 0)


def _wait_row_gather(n_rows, src_hbm, dst_ref, sem):
    pltpu.make_async_copy(src_hbm.at[pl.ds(0, n_rows), :], dst_ref, sem).wait()


def _expert_kernel(be_ref, bv_ref, tok_ref, tok_next_ref, x_hbm, wg_ref, wu_ref, bg_ref, bu_ref, wd_ref, bd_ref,
                   o_ref, xs_ref, xb_ref, sem, *, tm, n_f, n_blk):
    b = pl.program_id(0)
    f = pl.program_id(1)
    valid = bv_ref[b] > 0
    slot = b % 2

    @pl.when(f == 0)
    def _():
        @pl.when(jnp.logical_and(b == 0, valid))
        def _():
            _start_row_gather(tok_ref, tm, x_hbm, xs_ref.at[0], sem.at[0])

        nxt = jnp.minimum(b + 1, n_blk - 1)

        @pl.when(jnp.logical_and(b + 1 < n_blk, bv_ref[nxt] > 0))
        def _():
            _start_row_gather(tok_next_ref, tm, x_hbm, xs_ref.at[1 - slot], sem.at[1 - slot])

        @pl.when(valid)
        def _():
            _wait_row_gather(tm, x_hbm, xs_ref.at[slot], sem.at[slot])
            xb_ref[...] = xs_ref[slot].astype(xb_ref.dtype)

        o_ref[...] = jnp.zeros_like(o_ref)

    @pl.when(valid)
    def _():
        xb = xb_ref[...]
        gate = jnp.dot(xb, wg_ref[0], preferred_element_type=jnp.float32) + bg_ref[0]
        up = jnp.dot(xb, wu_ref[0], preferred_element_type=jnp.float32) + bu_ref[0]
        gate = jnp.minimum(gate, SWIGLU_LIMIT)
        up = jnp.clip(up, -SWIGLU_LIMIT, SWIGLU_LIMIT)
        hh = (up + 1.0) * gate * jax.nn.sigmoid(SWIGLU_ALPHA * gate)
        o_ref[...] += jnp.dot(hh.astype(xb.dtype), wd_ref[0], preferred_element_type=jnp.float32)

    @pl.when(jnp.logical_and(valid, f == n_f - 1))
    def _():
        o_ref[...] += bd_ref[0]


def _experts(xn, blk_expert, blk_valid, tok_rows, w_gu, b_gu, w_down, b_down, tm):
    n_blk = blk_expert.shape[0]
    e, d, two_f = w_gu.shape
    dff = two_f // 2
    tf = 1024 if dff % 1024 == 0 else dff
    n_f = dff // tf

    def fidx(f, bv, b):
        return jnp.where(bv[b] > 0, f, n_f - 1)

    grid_spec = pltpu.PrefetchScalarGridSpec(
        num_scalar_prefetch=2,
        grid=(n_blk, n_f),
        in_specs=[
            pl.BlockSpec((1, 1, tm), lambda b, f, be, bv: (b, 0, 0), memory_space=pltpu.SMEM),
            pl.BlockSpec((1, 1, tm), lambda b, f, be, bv: (jnp.minimum(b + 1, n_blk - 1), 0, 0),
                         memory_space=pltpu.SMEM),
            pl.BlockSpec(memory_space=pl.ANY),
            pl.BlockSpec((1, d, tf), lambda b, f, be, bv: (be[b], 0, fidx(f, bv, b))),
            pl.BlockSpec((1, d, tf), lambda b, f, be, bv: (be[b], 0, n_f + fidx(f, bv, b))),
            pl.BlockSpec((1, 1, tf), lambda b, f, be, bv: (be[b], 0, fidx(f, bv, b))),
            pl.BlockSpec((1, 1, tf), lambda b, f, be, bv: (be[b], 0, n_f + fidx(f, bv, b))),
            pl.BlockSpec((1, tf, d), lambda b, f, be, bv: (be[b], fidx(f, bv, b), 0)),
            pl.BlockSpec((1, 1, d), lambda b, f, be, bv: (be[b], 0, 0)),
        ],
        out_specs=pl.BlockSpec((tm, d), lambda b, f, be, bv: (b, 0)),
        scratch_shapes=[
            pltpu.VMEM((2, tm, d), jnp.float32),
            pltpu.VMEM((tm, d), jnp.bfloat16),
            pltpu.SemaphoreType.DMA((2,)),
        ],
    )
    tok3 = tok_rows.reshape(n_blk, 1, tm)
    return pl.pallas_call(
        functools.partial(_expert_kernel, tm=tm, n_f=n_f, n_blk=n_blk),
        grid_spec=grid_spec,
        out_shape=jax.ShapeDtypeStruct((n_blk * tm, d), jnp.float32),
        compiler_params=_cparams(("arbitrary", "arbitrary")),
        name="experts",
    )(blk_expert, blk_valid, tok3, tok3, xn, w_gu, w_gu,
      b_gu.reshape(e, 1, two_f), b_gu.reshape(e, 1, two_f), w_down, b_down.reshape(e, 1, d))


def _combine_kernel(pos_ref, pos_next_ref, h_ref, g_ref, ys_hbm, o_ref, buf_ref, sem, *, tt, nt):
    i = pl.program_id(0)
    slot = i % 2
    n_rows = TOP_K * tt

    @pl.when(i == 0)
    def _():
        _start_row_gather(pos_ref, n_rows, ys_hbm, buf_ref.at[0], sem.at[0])

    @pl.when(i + 1 < nt)
    def _():
        _start_row_gather(pos_next_ref, n_rows, ys_hbm, buf_ref.at[1 - slot], sem.at[1 - slot])

    _wait_row_gather(n_rows, ys_hbm, buf_ref.at[slot], sem.at[slot])
    g = g_ref[...]
    y = h_ref[...]
    for k in range(TOP_K):
        y = y + buf_ref[slot, k * tt:(k + 1) * tt, :] * g[:, k:k + 1]
    o_ref[...] = y


def _combine(h, gates, pos, ys):
    n, d = h.shape
    tt = _pick(n, (384, 256, 128, 64))
    nt = n // tt
    pos_t = pos.reshape(nt, tt, TOP_K).transpose(0, 2, 1).reshape(nt, 1, TOP_K * tt)
    row = lambda i: (i, 0)
    return pl.pallas_call(
        functools.partial(_combine_kernel, tt=tt, nt=nt),
        grid=(nt,),
        in_specs=[
            pl.BlockSpec((1, 1, TOP_K * tt), lambda i: (i, 0, 0), memory_space=pltpu.SMEM),
            pl.BlockSpec((1, 1, TOP_K * tt), lambda i: (jnp.minimum(i + 1, nt - 1), 0, 0), memory_space=pltpu.SMEM),
            pl.BlockSpec((tt, d), row),
            pl.BlockSpec((tt, LANES), row),
            pl.BlockSpec(memory_space=pl.ANY),
        ],
        out_specs=pl.BlockSpec((tt, d), row),
        out_shape=jax.ShapeDtypeStruct((n, d), jnp.float32),
        scratch_shapes=[pltpu.VMEM((2, TOP_K * tt, d), jnp.float32), pltpu.SemaphoreType.DMA((2,))],
        compiler_params=_cparams(("arbitrary",)),
        name="moe_combine",
    )(pos_t, pos_t, h, gates, ys)


def _moe_layout(top_i, n, tm):
    i32 = jnp.int32
    flat_e = top_i.reshape(-1)
    n_pairs = flat_e.shape[0]
    eids = jnp.arange(N_EXPERTS, dtype=i32)
    counts = jnp.sum(flat_e[:, None] == eids[None, :], axis=0, dtype=i32)
    nblk_e = (counts + tm - 1) // tm
    blk_end = jnp.cumsum(nblk_e)
    n_blk = (n_pairs + N_EXPERTS * (tm - 1)) // tm + 1
    n_rows = n_blk * tm
    blk = jnp.arange(n_blk, dtype=i32)
    blk_valid = (blk < blk_end[-1]).astype(i32)
    blk_expert = jnp.minimum(jnp.sum(blk[:, None] >= blk_end[None, :], axis=1, dtype=i32), N_EXPERTS - 1)
    last_e = jnp.max(jnp.where(nblk_e > 0, eids, 0))
    blk_expert = jnp.where(blk_valid > 0, blk_expert, last_e)
    fill_end = jnp.cumsum(nblk_e * tm - counts)
    filler = jnp.arange(n_rows - n_pairs, dtype=i32)
    fill_e = jnp.sum(filler[:, None] >= fill_end[None, :], axis=1, dtype=i32)
    key = jnp.concatenate([flat_e * 2, fill_e * 2 + 1])
    tok = jnp.concatenate([jnp.arange(n_pairs, dtype=i32) // TOP_K, jnp.zeros_like(filler)])
    _, tok_rows, src = lax.sort((key, tok, jnp.arange(n_rows, dtype=i32)), num_keys=1)
    _, row_of = lax.sort((src, jnp.arange(n_rows, dtype=i32)), num_keys=1)
    return blk_expert, blk_valid, tok_rows, row_of[:n_pairs].reshape(n, TOP_K)


def _pack_in_proj(w_in, d_model):
    hk = d_model // 2
    hv = d_model
    o = 0
    cuts = {}
    for name, width in (("gq", hk), ("gk", hk), ("gv", hv), ("gr", hv), ("ga", GLA_RANK), ("aq", d_model),
                        ("ak", d_model // GROUP), ("av", d_model // GROUP), ("iq", IDX_HEADS * IDX_DIM),
                        ("ik", IDX_DIM), ("iw", IDX_HEADS), ("za", d_model), ("zb", d_model)):
        cuts[name] = (o, o + width)
        o += width
    assert o == w_in.shape[1]
    col = lambda a, b: w_in[:, cuts[a][0]:cuts[b][1]]
    w1 = col("gq", "gr")
    w2 = col("aq", "av")
    w4 = col("za", "zb")
    k = w_in.shape[0]
    iq = jnp.pad(col("iq", "iq").reshape(k, IDX_HEADS, IDX_DIM), ((0, 0), (0, 0), (0, LANES - IDX_DIM)))
    pad_to = lambda a: jnp.pad(a, ((0, 0), (0, LANES - a.shape[1])))
    w3 = jnp.concatenate([iq.reshape(k, IDX_HEADS * LANES), pad_to(col("ik", "ik")), pad_to(col("ga", "ga")),
                          pad_to(col("iw", "iw"))], axis=1)
    bf = lambda a: a.astype(jnp.bfloat16)
    return bf(w1), bf(w2), bf(w3), bf(w4)


def kernel(x_prompt, x_sample, cache_k, cache_v, cache_kidx, state_gla, norm1_g, w_in, w_alpha2, b_alpha, gla_norm_g, q_norm_g, k_norm_g, w_proj_a, w_proj_b, w_out, norm2_g, w_router, b_router, w_gu, b_gu, w_down, b_down):
    B, T, D = x_prompt.shape
    DB, DS, _ = x_sample.shape
    depth = norm1_g.shape[0]
    NP, NS = B * T, DB * DS
    N = NP + NS
    past = cache_k.shape[2]
    hd = D // ATT_HEADS
    dk, dv = D // (2 * GLA_HEADS), D // GLA_HEADS
    kvw = ATT_KV_HEADS * hd
    bf16 = jnp.bfloat16

    pos = jnp.concatenate([jnp.tile(jnp.arange(T, dtype=jnp.int32), B),
                           jnp.tile(past + jnp.arange(DS, dtype=jnp.int32), DB)])
    x = jnp.concatenate([x_prompt.reshape(NP, D), x_sample.reshape(NS, D)], axis=0)
    outs = {k: [] for k in ("kp", "vp", "kip", "sp", "ks", "vs", "kis", "ss")}

    for l in range(depth):
        w1, w2, w3, w4 = _pack_in_proj(w_in[l], D)
        xn = _rmsnorm(x, norm1_g[l], bf16)
        z1 = _matmul(xn, w1, name="in_proj_gla")
        z2 = _matmul(xn, w2, name="in_proj_attn")
        z3 = _matmul(xn, w3, name="in_proj_idx")
        z4 = _matmul(xn, w4, name="in_proj_gate")

        w2p = jnp.pad(w_alpha2[l], ((0, LANES - GLA_RANK), (0, 0)))
        w2t = w2p.T
        oa_p, S_p = _gla(z1, z3, 0, B, T, w2p, w2t, b_alpha[l], gla_norm_g[l], None, dk, dv)
        oa_s, S_s = _gla(z1, z3, NP, DB, DS, w2p, w2t, b_alpha[l], gla_norm_g[l], state_gla[l], dk, dv)
        o_a = jnp.concatenate([oa_p, oa_s], axis=0)

        q_hm, k_f, k_b, v_b, qc_hm, ki_f, kc, w_pad = _prep(z2, z3, q_norm_g[l], k_norm_g[l], pos, hd)
        v_f = z2[:, ATT_HEADS * hd + kvw:]
        topk_p = min(TOPK_MAX, T // 4)
        tk_p = 1024 if T % 1024 == 0 else (256 if T % 256 == 0 else T)
        ob_p = _sparse_attention(q_hm[:, :NP], qc_hm[:, :NP], w_pad[:NP], kc[:NP].reshape(B, T, -1),
                                 k_b[:NP].reshape(B, T, kvw), v_b[:NP].reshape(B, T, kvw),
                                 B, T, T, 0, topk_p, 128, tk_p)
        lk_s = past + DS
        lk_pad = -(-lk_s // LANES) * LANES
        padk = lambda a: jnp.pad(a, ((0, 0), (0, lk_pad - lk_s), (0, 0)))
        ck_hi = cache_kidx[l].astype(bf16)
        ck_lo = (cache_kidx[l] - ck_hi.astype(jnp.float32)).astype(bf16)
        kc_s = padk(jnp.concatenate([jnp.concatenate([ck_hi, ck_hi, ck_lo, ck_lo], axis=-1),
                                     kc[NP:].reshape(DB, DS, -1)], axis=1))
        kb_s = padk(jnp.concatenate([cache_k[l].reshape(DB, past, kvw).astype(bf16),
                                     k_b[NP:].reshape(DB, DS, kvw)], axis=1))
        vb_s = padk(jnp.concatenate([cache_v[l].reshape(DB, past, kvw).astype(bf16),
                                     v_b[NP:].reshape(DB, DS, kvw)], axis=1))
        topk_s = min(TOPK_MAX, lk_s // 4)
        ob_s = _sparse_attention(q_hm[:, NP:], qc_hm[:, NP:], w_pad[NP:], kc_s, kb_s, vb_s,
                                 DB, DS, lk_s, past, topk_s, DS, lk_pad)
        o_b = jnp.concatenate([ob_p, ob_s], axis=0)

        mixed = _merge(o_a, o_b, w_proj_a[l].astype(bf16), w_proj_b[l].astype(bf16), z4)
        h = _resid_matmul(x, mixed, w_out[l].astype(bf16))

        xn2, top_i, gates = _router(h, norm2_g[l], w_router[l], b_router[l])
        tm_e = 512 if N >= 8192 else 64
        blk_e, blk_v, tok_rows, pos_rows = _moe_layout(top_i[:, :TOP_K], N, tm_e)
        ys = _experts(xn2, blk_e, blk_v, tok_rows, w_gu[l].astype(bf16), b_gu[l], w_down[l].astype(bf16),
                      b_down[l], tm_e)
        x = _combine(h, gates, pos_rows, ys)

        outs["kp"].append(k_f[:NP].reshape(B, T, ATT_KV_HEADS, hd))
        outs["vp"].append(v_f[:NP].reshape(B, T, ATT_KV_HEADS, hd))
        outs["kip"].append(ki_f[:NP].reshape(B, T, IDX_DIM))
        outs["sp"].append(S_p)
        outs["ks"].append(k_f[NP:].reshape(DB, DS, ATT_KV_HEADS, hd))
        outs["vs"].append(v_f[NP:].reshape(DB, DS, ATT_KV_HEADS, hd))
        outs["kis"].append(ki_f[NP:].reshape(DB, DS, IDX_DIM))
        outs["ss"].append(S_s)

    st = lambda k: jnp.stack(outs[k], axis=0)
    return (x[:NP].reshape(B, T, D), x[NP:].reshape(DB, DS, D), st("kp"), st("vp"), st("kip"), st("sp"),
            st("ks"), st("vs"), st("kis"), st("ss"))
```

```python
import functools
import math

import jax
import jax.numpy as jnp
from jax import lax
from jax.experimental import pallas as pl
from jax.experimental.pallas import tpu as pltpu

CHUNK = 64
GLA_HEADS = 4
GLA_RANK = 16
GLA_TAU = 16.0
ATT_HEADS = 16
ATT_KV_HEADS = 4
GROUP = ATT_HEADS // ATT_KV_HEADS
ROPE_THETA = 500000.0
IDX_HEADS = 8
IDX_DIM = 64
TOPK_MAX = 256
N_EXPERTS = 32
TOP_K = 4
SWIGLU_LIMIT = 7.0
SWIGLU_ALPHA = 1.702
EPS = 1e-6

LANES = 128
VMEM_LIMIT = 56 * 1024 * 1024
INT_MIN = -(2 ** 31)
ATTN_ROW_CHUNK = 16
SCORE_COLS = 256
LOG2E = 1.4426950408889634
NEG_INIT = -1e30
MASK_BIG = 2.0 ** 100
M_INIT = -(2.0 ** 120)

_HIGHEST = lax.Precision.HIGHEST
_NT = (((1,), (1,)), ((), ()))
_TN = (((0,), (0,)), ((), ()))


def _pick(n, cands):
    for c in cands:
        if n % c == 0:
            return c
    raise ValueError(f"no tile in {cands} divides {n}")


def _cparams(sem):
    return pltpu.CompilerParams(dimension_semantics=sem, vmem_limit_bytes=VMEM_LIMIT)


def _rmsnorm_kernel(x_ref, g_ref, o_ref):
    x = x_ref[...]
    y = x * lax.rsqrt(jnp.mean(x * x, axis=-1, keepdims=True) + EPS) * g_ref[...]
    o_ref[...] = y.astype(o_ref.dtype)


def _rmsnorm(x, g, out_dtype):
    n, d = x.shape
    tm = _pick(n, (768, 512, 256, 128, 64))
    return pl.pallas_call(
        _rmsnorm_kernel,
        grid=(n // tm,),
        in_specs=[pl.BlockSpec((tm, d), lambda i: (i, 0)), pl.BlockSpec((1, d), lambda i: (0, 0))],
        out_specs=pl.BlockSpec((tm, d), lambda i: (i, 0)),
        out_shape=jax.ShapeDtypeStruct((n, d), out_dtype),
        compiler_params=_cparams(("parallel",)),
        name="rmsnorm",
    )(x, g.reshape(1, d))


def _mm_kernel(a_ref, w_ref, o_ref):
    o_ref[...] = jnp.dot(a_ref[...], w_ref[...], preferred_element_type=jnp.float32).astype(o_ref.dtype)


def _matmul(a, w, out_dtype=jnp.float32, name="matmul"):
    m, k = a.shape
    _, n = w.shape
    tm = _pick(m, (768, 512, 256, 128, 64))
    tn = 1024 if n % 1024 == 0 else (512 if n % 512 == 0 else n)
    return pl.pallas_call(
        _mm_kernel,
        grid=(m // tm, n // tn),
        in_specs=[pl.BlockSpec((tm, k), lambda i, j: (i, 0)), pl.BlockSpec((k, tn), lambda i, j: (0, j))],
        out_specs=pl.BlockSpec((tm, tn), lambda i, j: (i, j)),
        out_shape=jax.ShapeDtypeStruct((m, n), out_dtype),
        compiler_params=_cparams(("parallel", "arbitrary")),
        name=name,
    )(a, w)


def _merge_kernel(oa_ref, ob_ref, wa_ref, wb_ref, za_ref, zb_ref, o_ref):
    pa = jnp.dot(oa_ref[...], wa_ref[...], preferred_element_type=jnp.float32)
    pb = jnp.dot(ob_ref[...], wb_ref[...], preferred_element_type=jnp.float32)
    o_ref[...] = (jax.nn.sigmoid(za_ref[...]) * pa + jax.nn.sigmoid(zb_ref[...]) * pb).astype(o_ref.dtype)


def _merge(o_a, o_b, wa, wb, z4):
    m, k = o_a.shape
    n = wa.shape[1]
    tm = _pick(m, (768, 512, 256, 128, 64))
    tn = 512
    nj = n // tn
    return pl.pallas_call(
        _merge_kernel,
        grid=(m // tm, nj),
        in_specs=[
            pl.BlockSpec((tm, k), lambda i, j: (i, 0)),
            pl.BlockSpec((tm, k), lambda i, j: (i, 0)),
            pl.BlockSpec((k, tn), lambda i, j: (0, j)),
            pl.BlockSpec((k, tn), lambda i, j: (0, j)),
            pl.BlockSpec((tm, tn), lambda i, j: (i, j)),
            pl.BlockSpec((tm, tn), lambda i, j: (i, j + nj)),
        ],
        out_specs=pl.BlockSpec((tm, tn), lambda i, j: (i, j)),
        out_shape=jax.ShapeDtypeStruct((m, n), jnp.bfloat16),
        compiler_params=_cparams(("parallel", "arbitrary")),
        name="merge",
    )(o_a, o_b, wa, wb, z4, z4)


def _resid_mm_kernel(x_ref, a_ref, w_ref, o_ref):
    o_ref[...] = x_ref[...] + jnp.dot(a_ref[...], w_ref[...], preferred_element_type=jnp.float32)


def _resid_matmul(x, a, w):
    m, k = a.shape
    n = w.shape[1]
    tm = _pick(m, (768, 512, 256, 128, 64))
    tn = 1024 if n % 1024 == 0 else n
    return pl.pallas_call(
        _resid_mm_kernel,
        grid=(m // tm, n // tn),
        in_specs=[
            pl.BlockSpec((tm, tn), lambda i, j: (i, j)),
            pl.BlockSpec((tm, k), lambda i, j: (i, 0)),
            pl.BlockSpec((k, tn), lambda i, j: (0, j)),
        ],
        out_specs=pl.BlockSpec((tm, tn), lambda i, j: (i, j)),
        out_shape=jax.ShapeDtypeStruct((m, n), jnp.float32),
        compiler_params=_cparams(("parallel", "arbitrary")),
        name="out_proj",
    )(x, a, w)


def _log_sigmoid(x):
    return jnp.minimum(x, 0.0) - jnp.log(1.0 + jnp.exp(-jnp.abs(x)))


def _gla_kernel(*refs, c, dk, dv, has_init):
    if has_init:
        (z1_ref, ga_ref, w2_ref, b_ref, gn_ref, s0_ref, o_ref, sout_ref, s_scr) = refs
    else:
        (z1_ref, ga_ref, w2_ref, b_ref, gn_ref, o_ref, sout_ref, s_scr) = refs
        s0_ref = None
    ci = pl.program_id(1)
    hk = GLA_HEADS * dk

    @pl.when(ci == 0)
    def _():
        if has_init:
            s_scr[...] = s0_ref[0]
        else:
            s_scr[...] = jnp.zeros_like(s_scr)

    ga = ga_ref[...]
    g = _log_sigmoid(jnp.dot(ga, w2_ref[...], precision=_HIGHEST,
                             preferred_element_type=jnp.float32) + b_ref[...]) / GLA_TAU
    row = lax.broadcasted_iota(jnp.int32, (c, c), 0)
    col = lax.broadcasted_iota(jnp.int32, (c, c), 1)
    causal = col <= row
    b = jnp.dot(causal.astype(jnp.float32), g, precision=_HIGHEST, preferred_element_type=jnp.float32)
    bend_col = jnp.broadcast_to(b[c - 1:c, :], (8, hk)).T[:, 0:1]
    scale = dk ** -0.5
    for h in range(GLA_HEADS):
        bh = b[:, h * dk:(h + 1) * dk]
        q = z1_ref[:, h * dk:(h + 1) * dk] * scale
        k = z1_ref[:, hk + h * dk: hk + (h + 1) * dk]
        v = z1_ref[:, 2 * hk + h * dv: 2 * hk + (h + 1) * dv].astype(jnp.bfloat16)
        r = z1_ref[:, 2 * hk + GLA_HEADS * dv + h * dv: 2 * hk + GLA_HEADS * dv + (h + 1) * dv]
        qe = (q * jnp.exp(bh)).astype(jnp.bfloat16)
        ke = (k * jnp.exp(-bh)).astype(jnp.bfloat16)
        a = lax.dot_general(qe, ke, _NT, preferred_element_type=jnp.float32)
        a = jnp.where(causal, a, 0.0).astype(jnp.bfloat16)
        s = s_scr[h]
        o = (jnp.dot(a, v, preferred_element_type=jnp.float32)
             + jnp.dot(qe, s.astype(jnp.bfloat16), preferred_element_type=jnp.float32))
        b_end = bh[c - 1:c, :]
        kd = (k * jnp.exp(b_end - bh)).astype(jnp.bfloat16)
        s_scr[h] = (jnp.exp(bend_col[h * dk:(h + 1) * dk, :]) * s
                    + lax.dot_general(kd, v, _TN, preferred_element_type=jnp.float32))
        on = o * lax.rsqrt(jnp.mean(o * o, axis=-1, keepdims=True) + EPS) * gn_ref[...]
        o_ref[:, h * dv:(h + 1) * dv] = (on * (r * jax.nn.sigmoid(r))).astype(o_ref.dtype)

    @pl.when(ci == pl.num_programs(1) - 1)
    def _():
        sout_ref[0] = s_scr[...]


def _gla(z1, z3, row0, n_seq, seq_len, w2p, b_alpha, gn, s0, dk, dv):
    c = CHUNK if seq_len % CHUNK == 0 else seq_len
    nc = seq_len // c
    hk = GLA_HEADS * dk
    blk0 = row0 // c
    ga_blk = (IDX_HEADS + 1)
    has_init = s0 is not None
    rows = n_seq * seq_len
    in_specs = [
        pl.BlockSpec((c, z1.shape[1]), lambda s, i: (blk0 + s * nc + i, 0)),
        pl.BlockSpec((c, LANES), lambda s, i: (blk0 + s * nc + i, ga_blk)),
        pl.BlockSpec((LANES, hk), lambda s, i: (0, 0)),
        pl.BlockSpec((1, hk), lambda s, i: (0, 0)),
        pl.BlockSpec((1, dv), lambda s, i: (0, 0)),
    ]
    args = [z1, z3, w2p, b_alpha.reshape(1, hk), gn.reshape(1, dv)]
    if has_init:
        in_specs.append(pl.BlockSpec((1, GLA_HEADS, dk, dv), lambda s, i: (s, 0, 0, 0)))
        args.append(s0)
    return pl.pallas_call(
        functools.partial(_gla_kernel, c=c, dk=dk, dv=dv, has_init=has_init),
        grid=(n_seq, nc),
        in_specs=in_specs,
        out_specs=[
            pl.BlockSpec((c, GLA_HEADS * dv), lambda s, i: (s * nc + i, 0)),
            pl.BlockSpec((1, GLA_HEADS, dk, dv), lambda s, i: (s, 0, 0, 0)),
        ],
        out_shape=[
            jax.ShapeDtypeStruct((rows, GLA_HEADS * dv), jnp.bfloat16),
            jax.ShapeDtypeStruct((n_seq, GLA_HEADS, dk, dv), jnp.float32),
        ],
        scratch_shapes=[pltpu.VMEM((GLA_HEADS, dk, dv), jnp.float32)],
        compiler_params=_cparams(("parallel", "arbitrary")),
        name="gla_init" if has_init else "gla",
    )(*args)


def _rope(x, cos, sin, half):
    lane = lax.broadcasted_iota(jnp.int32, x.shape, 1)
    partner = jnp.where(lane < half, pltpu.roll(x, LANES - half, 1), pltpu.roll(x, half, 1))
    return x * cos + partner * sin


def _split_hi_lo(x):
    hi = x.astype(jnp.bfloat16).astype(jnp.float32)
    lo = (x - hi).astype(jnp.bfloat16).astype(jnp.float32)
    return hi, lo


def _prep_kernel(z2_ref, z3_ref, qg_ref, kg_ref, ca_ref, sa_ref, ci_ref, si_ref,
                 q_ref, kf_ref, kb_ref, vb_ref, qc_ref, kif_ref, kc_ref, w_ref, *, hd):
    ca, sa, ci, si = ca_ref[...], sa_ref[...], ci_ref[...], si_ref[...]
    qscale = hd ** -0.5 * LOG2E

    def headnorm(x, g):
        return x * lax.rsqrt(jnp.mean(x * x, axis=-1, keepdims=True) + EPS) * g

    for h in range(ATT_HEADS):
        x = headnorm(z2_ref[:, h * hd:(h + 1) * hd], qg_ref[...])
        q_ref[h] = (_rope(x, ca, sa, hd // 8) * qscale).astype(q_ref.dtype)
    k0 = ATT_HEADS * hd
    for h in range(ATT_KV_HEADS):
        x = headnorm(z2_ref[:, k0 + h * hd:k0 + (h + 1) * hd], kg_ref[...])
        y = _rope(x, ca, sa, hd // 8)
        kf_ref[:, h * hd:(h + 1) * hd] = y
        kb_ref[:, h * hd:(h + 1) * hd] = y.astype(kb_ref.dtype)
    v0 = k0 + ATT_KV_HEADS * hd
    vb_ref[...] = z2_ref[:, v0:v0 + ATT_KV_HEADS * hd].astype(vb_ref.dtype)
    for h in range(IDX_HEADS):
        y = _rope(z3_ref[:, h * LANES:(h + 1) * LANES], ci, si, IDX_DIM // 8)
        hi, lo = _split_hi_lo(y)
        t = (hi + pltpu.roll(lo, IDX_DIM, 1)).astype(qc_ref.dtype)
        qc_ref[h] = jnp.concatenate([t, t], axis=1)
    y = _rope(z3_ref[:, IDX_HEADS * LANES:(IDX_HEADS + 1) * LANES], ci, si, IDX_DIM // 8)
    kif_ref[...] = y[:, :IDX_DIM]
    hi, lo = _split_hi_lo(y)
    kc_ref[...] = jnp.concatenate([hi + pltpu.roll(hi, IDX_DIM, 1), lo + pltpu.roll(lo, IDX_DIM, 1)],
                                  axis=1).astype(kc_ref.dtype)
    w_ref[...] = z3_ref[:, (IDX_HEADS + 2) * LANES:(IDX_HEADS + 3) * LANES] * ((IDX_HEADS * IDX_DIM) ** -0.5)


def _rope_tables(pos, rot):
    half = rot // 2
    inv = ROPE_THETA ** (-2.0 * jnp.arange(half, dtype=jnp.float32) / rot)
    ang = pos.astype(jnp.float32)[:, None] * inv[None, :]
    cos, sin = jnp.cos(ang), jnp.sin(ang)
    n = pos.shape[0]
    ones = jnp.ones((n, LANES - rot), jnp.float32)
    zeros = jnp.zeros((n, LANES - rot), jnp.float32)
    return (jnp.concatenate([cos, cos, ones], axis=1), jnp.concatenate([-sin, sin, zeros], axis=1))


def _prep(z2, z3, q_norm_g, k_norm_g, pos, hd):
    n = z2.shape[0]
    tr = _pick(n, (256, 128, 64))
    ca, sa = _rope_tables(pos, hd // 4)
    ci, si = _rope_tables(pos, IDX_DIM // 4)
    row = lambda i: (i, 0)
    tab = pl.BlockSpec((tr, LANES), row)
    kvw = ATT_KV_HEADS * hd
    return pl.pallas_call(
        functools.partial(_prep_kernel, hd=hd),
        grid=(n // tr,),
        in_specs=[
            pl.BlockSpec((tr, z2.shape[1]), row),
            pl.BlockSpec((tr, z3.shape[1]), row),
            pl.BlockSpec((1, hd), lambda i: (0, 0)),
            pl.BlockSpec((1, hd), lambda i: (0, 0)),
            tab, tab, tab, tab,
        ],
        out_specs=[
            pl.BlockSpec((ATT_HEADS, tr, hd), lambda i: (0, i, 0)),
            pl.BlockSpec((tr, kvw), row),
            pl.BlockSpec((tr, kvw), row),
            pl.BlockSpec((tr, kvw), row),
            pl.BlockSpec((IDX_HEADS, tr, 4 * IDX_DIM), lambda i: (0, i, 0)),
            pl.BlockSpec((tr, IDX_DIM), row),
            pl.BlockSpec((tr, 4 * IDX_DIM), row),
            pl.BlockSpec((tr, LANES), row),
        ],
        out_shape=[
            jax.ShapeDtypeStruct((ATT_HEADS, n, hd), jnp.bfloat16),
            jax.ShapeDtypeStruct((n, kvw), jnp.float32),
            jax.ShapeDtypeStruct((n, kvw), jnp.bfloat16),
            jax.ShapeDtypeStruct((n, kvw), jnp.bfloat16),
            jax.ShapeDtypeStruct((IDX_HEADS, n, 4 * IDX_DIM), jnp.bfloat16),
            jax.ShapeDtypeStruct((n, IDX_DIM), jnp.float32),
            jax.ShapeDtypeStruct((n, 4 * IDX_DIM), jnp.bfloat16),
            jax.ShapeDtypeStruct((n, LANES), jnp.float32),
        ],
        compiler_params=_cparams(("parallel",)),
        name="attn_prep",
    )(z2, z3, q_norm_g.reshape(1, hd), k_norm_g.reshape(1, hd), ca, sa, ci, si)


def _needed_tiles(i, tq, tk, lk_true, pos_off):
    last = i * tq + pos_off + tq - 1
    max_limit = jnp.minimum((last // CHUNK + 1) * CHUNK, lk_true)
    return (max_limit + tk - 1) // tk


def _attn_kernel(qi_ref, kt_ref, q_ref, qc_ref, w_ref, kc_ref, k_ref, v_ref, o_ref,
                 keys_ref, hi_ref, lo_ref, thr_ref, s_ref, p_ref, alpha_ref, m_ref, acc_ref,
                 *, tq, tk, lk_true, pos_off, topk, nbits, hd):
    step = pl.program_id(1)
    i = qi_ref[step]
    kt = kt_ref[step]
    n_need = _needed_tiles(i, tq, tk, lk_true, pos_off)
    n_sub = tk // LANES
    rows = GROUP * tq
    rc = min(ATTN_ROW_CHUNK, tq)

    def tile_keys(j):
        return jnp.concatenate([keys_ref[j * n_sub + c] for c in range(n_sub)], axis=1)

    @pl.when(kt == 0)
    def _select():
        rowi = lax.broadcasted_iota(jnp.int32, (tq, 1), 0)
        limit = jnp.minimum(((i * tq + pos_off + rowi) // CHUNK + 1) * CHUNK, lk_true)
        k_eff = jnp.minimum(topk, limit).astype(jnp.float32)
        w = w_ref[...]
        qc_all = qc_ref[...].reshape(IDX_HEADS * tq, 4 * IDX_DIM)
        cw = SCORE_COLS if tk % SCORE_COLS == 0 else tk

        def score_body(j, carry):
            for cs in range(tk // cw):
                cols = pl.ds(pl.multiple_of(j * tk + cs * cw, cw), cw)
                s_all = lax.dot_general(qc_all, kc_ref[cols, :], _NT, preferred_element_type=jnp.float32)
                acc = jnp.zeros((tq, cw), jnp.float32)
                for h in range(IDX_HEADS):
                    acc = acc + jnp.maximum(s_all[h * tq:(h + 1) * tq], 0.0) * w[:, h:h + 1]
                bits = pltpu.bitcast(acc, jnp.int32)
                sgn = bits >> 31
                key = (bits ^ (sgn & 0x7FFFFFFF)) - sgn
                col = j * tk + cs * cw + lax.broadcasted_iota(jnp.int32, (tq, cw), 1)
                key = jnp.where(col < limit, key, INT_MIN)
                hi = (key >> 16).astype(jnp.int16)
                for c in range(cw // LANES):
                    slab = j * n_sub + cs * (cw // LANES) + c
                    keys_ref[slab] = key[:, c * LANES:(c + 1) * LANES]
                    hi_ref[slab] = hi[:, c * LANES:(c + 1) * LANES]
            return carry

        lax.fori_loop(0, n_need, score_body, 0)

        def count(pred):
            def body(j, acc):
                for c in range(n_sub):
                    kk = keys_ref[j * n_sub + c]
                    col = j * tk + c * LANES + lax.broadcasted_iota(jnp.int32, (tq, LANES), 1)
                    acc = acc + jnp.where(pred(kk, col), 1.0, 0.0)
                return acc
            acc = lax.fori_loop(0, n_need, body, jnp.zeros((tq, LANES), jnp.float32))
            return jnp.sum(acc, axis=1, keepdims=True)

        def count16(ref, pred):
            one = jnp.ones((tq, LANES), jnp.int16)
            zero = jnp.zeros((tq, LANES), jnp.int16)

            def body(j, acc):
                for c in range(n_sub):
                    acc = acc + jnp.where(pred(ref[j * n_sub + c]), one, zero)
                return acc
            acc = lax.fori_loop(0, n_need, body, zero)
            return jnp.sum(acc.astype(jnp.float32), axis=1, keepdims=True)

        def as16(t):
            return jnp.broadcast_to(t, (tq, LANES)).astype(jnp.int16)

        def search16(ref, target):
            def bit_body(b, cand):
                t = cand + lax.shift_left(jnp.int32(1), 15 - b)
                t16 = as16(t)
                c = count16(ref, lambda kk: kk >= t16)
                return jnp.where(c >= target, t, cand)
            return lax.fori_loop(0, 16, bit_body, jnp.full((tq, 1), -32768, jnp.int32))

        thr_hi = search16(hi_ref, k_eff)
        thr_hi16 = as16(thr_hi)
        k_lo = k_eff - count16(hi_ref, lambda kk: kk > thr_hi16)

        def lo_body(j, carry):
            for c in range(n_sub):
                key = keys_ref[j * n_sub + c]
                lo = jnp.where((key >> 16) == thr_hi, (key & 0xFFFF) - 32768, -32768)
                lo_ref[j * n_sub + c] = lo.astype(jnp.int16)
            return carry

        lax.fori_loop(0, n_need, lo_body, 0)
        thr_lo = search16(lo_ref, k_lo)
        thr = thr_hi * 65536 + (thr_lo + 32768)
        thr_ref[...] = thr
        c_gt = count(lambda kk, col: kk > thr)
        c_ge = count(lambda kk, col: kk >= thr)
        need = k_eff - c_gt
        excess = jnp.max(c_ge - k_eff) > 0.5

        @pl.when(excess)
        def _ties():
            def idx_body(b, cut):
                t = cut + lax.shift_left(jnp.int32(1), nbits - 1 - b)
                c = count(lambda kk, col: jnp.where(kk == thr, col, t) < t)
                return jnp.where(c <= need, t, cut)

            cut = lax.fori_loop(0, nbits, idx_body, jnp.zeros((tq, 1), jnp.int32))

            def fix_body(j, carry):
                for c in range(n_sub):
                    kk = keys_ref[j * n_sub + c]
                    col = j * tk + c * LANES + lax.broadcasted_iota(jnp.int32, (tq, LANES), 1)
                    drop = jnp.where(kk == thr, col, -1) >= cut
                    keys_ref[j * n_sub + c] = jnp.where(drop, INT_MIN, kk)
                return carry

            lax.fori_loop(0, n_need, fix_body, 0)

        m_ref[...] = jnp.full(m_ref.shape, M_INIT, jnp.float32)
        acc_ref[...] = jnp.zeros(acc_ref.shape, jnp.float32)

    def _attend():
        sel = tile_keys(kt) >= thr_ref[...]
        mask = jnp.where(sel, 0.0, -MASK_BIG)
        if tq < LANES:
            mask = jnp.concatenate([mask, jnp.zeros((LANES - tq, tk), jnp.float32)], axis=0)
        mask_t = mask.T.astype(k_ref.dtype)
        rowi = lax.broadcasted_iota(jnp.int32, (tq, LANES), 0)
        lanei = lax.broadcasted_iota(jnp.int32, (tq, LANES), 1)
        eye = jnp.where(rowi == lanei, 1.0, 0.0).astype(q_ref.dtype)
        eye4 = jnp.concatenate([eye] * GROUP, axis=0)

        def qk(kvh):
            q4 = jnp.concatenate([q_ref[kvh * GROUP:(kvh + 1) * GROUP].reshape(rows, hd), eye4], axis=1)
            kb = jnp.concatenate([k_ref[:, kvh * hd:(kvh + 1) * hd], mask_t], axis=1)
            s_ref[kvh % 2] = lax.dot_general(q4, kb, _NT, preferred_element_type=jnp.float32)

        qk(0)
        for kvh in range(ATT_KV_HEADS):
            if kvh + 1 < ATT_KV_HEADS:
                qk(kvh + 1)
            sb = kvh % 2
            for c in range(rows // rc):
                r0 = c * rc
                t = s_ref[sb, r0:r0 + rc, :]
                m_prev = m_ref[kvh, r0:r0 + rc, :]
                m_new = jnp.maximum(m_prev, jnp.max(t, axis=1, keepdims=True))
                p_ref[sb, r0:r0 + rc, :] = jnp.exp2((t - m_new).astype(p_ref.dtype))
                alpha_ref[sb, r0:r0 + rc, :] = jnp.exp2(m_prev - m_new)
                m_ref[kvh, r0:r0 + rc, :] = m_new
            vb = jnp.concatenate([v_ref[:, kvh * hd:(kvh + 1) * hd], jnp.ones((tk, LANES), v_ref.dtype)], axis=1)
            acc_ref[kvh] = alpha_ref[sb] * acc_ref[kvh] + jnp.dot(p_ref[sb], vb, preferred_element_type=jnp.float32)

    _attend()

    @pl.when(kt == n_need - 1)
    def _finish():
        for kvh in range(ATT_KV_HEADS):
            o = acc_ref[kvh, :, 0:hd] / acc_ref[kvh, :, hd:hd + 1]
            for g in range(GROUP):
                h = kvh * GROUP + g
                o_ref[:, h * hd:(h + 1) * hd] = o[g * tq:(g + 1) * tq].astype(o_ref.dtype)


def _sparse_attention(q_hm, qcat_hm, w_pad, kcat, kb, vb, n_grp, tq_total, lk_pad, lk_true, pos_off, topk, tq, tk,
                      q_row0):
    hd = q_hm.shape[-1]
    qb0 = q_row0 // tq
    n_kt = lk_pad // tk
    nq = tq_total // tq
    nbits = max(1, math.ceil(math.log2(lk_pad + 1)))
    pairs = [(i, kt) for i in range(nq)
             for kt in range(-(-min(((i * tq + pos_off + tq - 1) // CHUNK + 1) * CHUNK, lk_true) // tk))]
    qi_of = jnp.asarray([p[0] for p in pairs], jnp.int32)
    kt_of = jnp.asarray([p[1] for p in pairs], jnp.int32)
    qblk = lambda g, s, qi, kt: (0, qb0 + g * nq + qi[s], 0)
    qrow = lambda g, s, qi, kt: (qb0 + g * nq + qi[s], 0)
    orow = lambda g, s, qi, kt: (g * nq + qi[s], 0)
    kv_map = lambda g, s, qi, kt: (g * n_kt + kt[s], 0)
    kern = functools.partial(_attn_kernel, tq=tq, tk=tk, lk_true=lk_true, pos_off=pos_off,
                             topk=topk, nbits=nbits, hd=hd)
    grid_spec = pltpu.PrefetchScalarGridSpec(
        num_scalar_prefetch=2,
        grid=(n_grp, len(pairs)),
        in_specs=[
            pl.BlockSpec((ATT_HEADS, tq, hd), qblk),
            pl.BlockSpec((IDX_HEADS, tq, 4 * IDX_DIM), qblk),
            pl.BlockSpec((tq, LANES), qrow),
            pl.BlockSpec((lk_pad, 4 * IDX_DIM), lambda g, s, qi, kt: (g, 0)),
            pl.BlockSpec((tk, ATT_KV_HEADS * hd), kv_map),
            pl.BlockSpec((tk, ATT_KV_HEADS * hd), kv_map),
        ],
        out_specs=pl.BlockSpec((tq, ATT_HEADS * hd), orow),
        scratch_shapes=[
            pltpu.VMEM((lk_pad // LANES, tq, LANES), jnp.int32),
            pltpu.VMEM((lk_pad // LANES, tq, LANES), jnp.int16),
            pltpu.VMEM((lk_pad // LANES, tq, LANES), jnp.int16),
            pltpu.VMEM((tq, 1), jnp.int32),
            pltpu.VMEM((2, GROUP * tq, tk), jnp.float32),
            pltpu.VMEM((2, GROUP * tq, tk), jnp.bfloat16),
            pltpu.VMEM((2, GROUP * tq, 1), jnp.float32),
            pltpu.VMEM((ATT_KV_HEADS, GROUP * tq, 1), jnp.float32),
            pltpu.VMEM((ATT_KV_HEADS, GROUP * tq, hd + LANES), jnp.float32),
        ],
    )
    return pl.pallas_call(
        kern,
        grid_spec=grid_spec,
        out_shape=jax.ShapeDtypeStruct((n_grp * tq_total, ATT_HEADS * hd), jnp.bfloat16),
        compiler_params=_cparams(("parallel", "arbitrary")),
        name="sparse_attn",
    )(qi_of, kt_of, q_hm, qcat_hm, w_pad, kcat, kb, vb)


def _router_kernel(h_ref, g_ref, wr_ref, br_ref, xn_ref, ti_ref, tg_ref):
    h = h_ref[...]
    xn = h * lax.rsqrt(jnp.mean(h * h, axis=-1, keepdims=True) + EPS) * g_ref[...]
    xn_ref[...] = xn
    logits = jnp.dot(xn, wr_ref[...], precision=_HIGHEST, preferred_element_type=jnp.float32) + br_ref[...]
    lane = lax.broadcasted_iota(jnp.int32, logits.shape, 1)
    cur = logits
    vals, idxs = [], []
    for _ in range(TOP_K):
        m = jnp.max(cur, axis=1, keepdims=True)
        idx = jnp.min(jnp.where(cur == m, lane, LANES), axis=1, keepdims=True)
        vals.append(m)
        idxs.append(idx)
        cur = jnp.where(lane == idx, -jnp.inf, cur)
    es = [jnp.exp(v - vals[0]) for v in vals]
    denom = es[0] + es[1] + es[2] + es[3]
    ti = jnp.zeros(logits.shape, jnp.int32)
    tg = jnp.zeros(logits.shape, jnp.float32)
    for k in range(TOP_K):
        ti = jnp.where(lane == k, idxs[k], ti)
        tg = jnp.where(lane == k, es[k] / denom, tg)
    ti_ref[...] = ti
    tg_ref[...] = tg


def _router(h, g, w_router, b_router):
    n, d = h.shape
    tm = _pick(n, (768, 512, 256, 128, 64))
    wr = jnp.pad(w_router, ((0, 0), (0, LANES - N_EXPERTS)))
    br = jnp.pad(b_router, (0, LANES - N_EXPERTS), constant_values=NEG_INIT).reshape(1, LANES)
    row = lambda i: (i, 0)
    return pl.pallas_call(
        _router_kernel,
        grid=(n // tm,),
        in_specs=[
            pl.BlockSpec((tm, d), row),
            pl.BlockSpec((1, d), lambda i: (0, 0)),
            pl.BlockSpec((d, LANES), lambda i: (0, 0)),
            pl.BlockSpec((1, LANES), lambda i: (0, 0)),
        ],
        out_specs=[pl.BlockSpec((tm, d), row), pl.BlockSpec((tm, LANES), row), pl.BlockSpec((tm, LANES), row)],
        out_shape=[
            jax.ShapeDtypeStruct((n, d), jnp.float32),
            jax.ShapeDtypeStruct((n, LANES), jnp.int32),
            jax.ShapeDtypeStruct((n, LANES), jnp.float32),
        ],
        compiler_params=_cparams(("parallel",)),
        name="router",
    )(h, g.reshape(1, d), wr, br)


def _start_row_gather(idx_ref, n_rows, src_hbm, dst_ref, sem):
    def start(r, c):
        pltpu.make_async_copy(src_hbm.at[pl.ds(idx_ref[0, 0, r], 1), :], dst_ref.at[pl.ds(r, 1), :], sem).start()
        return c

    lax.fori_loop(0, n_rows, start, 0, unroll=8)


def _wait_row_gather(n_rows, src_hbm, dst_ref, sem):
    pltpu.make_async_copy(src_hbm.at[pl.ds(0, n_rows), :], dst_ref, sem).wait()


def _expert_kernel(be_ref, bv_ref, tok_ref, tok_next_ref, x_hbm, wg_ref, wu_ref, bg_ref, bu_ref, wd_ref, bd_ref,
                   o_ref, xs_ref, xb_ref, sem, *, tm, n_f, n_blk):
    b = pl.program_id(0)
    f = pl.program_id(1)
    valid = bv_ref[b] > 0
    slot = b % 2

    @pl.when(f == 0)
    def _():
        @pl.when(jnp.logical_and(b == 0, valid))
        def _():
            _start_row_gather(tok_ref, tm, x_hbm, xs_ref.at[0], sem.at[0])

        nxt = jnp.minimum(b + 1, n_blk - 1)

        @pl.when(jnp.logical_and(b + 1 < n_blk, bv_ref[nxt] > 0))
        def _():
            _start_row_gather(tok_next_ref, tm, x_hbm, xs_ref.at[1 - slot], sem.at[1 - slot])

        @pl.when(valid)
        def _():
            _wait_row_gather(tm, x_hbm, xs_ref.at[slot], sem.at[slot])
            xb_ref[...] = xs_ref[slot].astype(xb_ref.dtype)

        o_ref[...] = jnp.zeros_like(o_ref)

    @pl.when(valid)
    def _():
        xb = xb_ref[...]
        gate = jnp.dot(xb, wg_ref[0], preferred_element_type=jnp.float32) + bg_ref[0]
        up = jnp.dot(xb, wu_ref[0], preferred_element_type=jnp.float32) + bu_ref[0]
        gate = jnp.minimum(gate, SWIGLU_LIMIT)
        up = jnp.clip(up, -SWIGLU_LIMIT, SWIGLU_LIMIT)
        hh = (up + 1.0) * gate * jax.nn.sigmoid(SWIGLU_ALPHA * gate)
        o_ref[...] += jnp.dot(hh.astype(xb.dtype), wd_ref[0], preferred_element_type=jnp.float32)

    @pl.when(jnp.logical_and(valid, f == n_f - 1))
    def _():
        o_ref[...] += bd_ref[0]


def _experts(xn, blk_expert, blk_valid, tok_rows, w_gu, b_gu, w_down, b_down, tm):
    n_blk = blk_expert.shape[0]
    e, d, two_f = w_gu.shape
    dff = two_f // 2
    tf = 1024 if dff % 1024 == 0 else dff
    n_f = dff // tf

    def fidx(f, bv, b):
        return jnp.where(bv[b] > 0, f, n_f - 1)

    grid_spec = pltpu.PrefetchScalarGridSpec(
        num_scalar_prefetch=2,
        grid=(n_blk, n_f),
        in_specs=[
            pl.BlockSpec((1, 1, tm), lambda b, f, be, bv: (b, 0, 0), memory_space=pltpu.SMEM),
            pl.BlockSpec((1, 1, tm), lambda b, f, be, bv: (jnp.minimum(b + 1, n_blk - 1), 0, 0),
                         memory_space=pltpu.SMEM),
            pl.BlockSpec(memory_space=pl.ANY),
            pl.BlockSpec((1, d, tf), lambda b, f, be, bv: (be[b], 0, fidx(f, bv, b))),
            pl.BlockSpec((1, d, tf), lambda b, f, be, bv: (be[b], 0, n_f + fidx(f, bv, b))),
            pl.BlockSpec((1, 1, tf), lambda b, f, be, bv: (be[b], 0, fidx(f, bv, b))),
            pl.BlockSpec((1, 1, tf), lambda b, f, be, bv: (be[b], 0, n_f + fidx(f, bv, b))),
            pl.BlockSpec((1, tf, d), lambda b, f, be, bv: (be[b], fidx(f, bv, b), 0)),
            pl.BlockSpec((1, 1, d), lambda b, f, be, bv: (be[b], 0, 0)),
        ],
        out_specs=pl.BlockSpec((tm, d), lambda b, f, be, bv: (b, 0)),
        scratch_shapes=[
            pltpu.VMEM((2, tm, d), jnp.float32),
            pltpu.VMEM((tm, d), jnp.bfloat16),
            pltpu.SemaphoreType.DMA((2,)),
        ],
    )
    tok3 = tok_rows.reshape(n_blk, 1, tm)
    return pl.pallas_call(
        functools.partial(_expert_kernel, tm=tm, n_f=n_f, n_blk=n_blk),
        grid_spec=grid_spec,
        out_shape=jax.ShapeDtypeStruct((n_blk * tm, d), jnp.float32),
        compiler_params=_cparams(("arbitrary", "arbitrary")),
        name="experts",
    )(blk_expert, blk_valid, tok3, tok3, xn, w_gu, w_gu,
      b_gu.reshape(e, 1, two_f), b_gu.reshape(e, 1, two_f), w_down, b_down.reshape(e, 1, d))


def _combine_kernel(pos_ref, pos_next_ref, h_ref, g_ref, ys_hbm, o_ref, buf_ref, sem, *, tt, nt):
    i = pl.program_id(0)
    slot = i % 2
    n_rows = TOP_K * tt

    @pl.when(i == 0)
    def _():
        _start_row_gather(pos_ref, n_rows, ys_hbm, buf_ref.at[0], sem.at[0])

    @pl.when(i + 1 < nt)
    def _():
        _start_row_gather(pos_next_ref, n_rows, ys_hbm, buf_ref.at[1 - slot], sem.at[1 - slot])

    _wait_row_gather(n_rows, ys_hbm, buf_ref.at[slot], sem.at[slot])
    g = g_ref[...]
    y = h_ref[...]
    for k in range(TOP_K):
        y = y + buf_ref[slot, k * tt:(k + 1) * tt, :] * g[:, k:k + 1]
    o_ref[...] = y


def _combine(h, gates, pos, ys):
    n, d = h.shape
    tt = _pick(n, (384, 256, 128, 64))
    nt = n // tt
    pos_t = pos.reshape(nt, tt, TOP_K).transpose(0, 2, 1).reshape(nt, 1, TOP_K * tt)
    row = lambda i: (i, 0)
    return pl.pallas_call(
        functools.partial(_combine_kernel, tt=tt, nt=nt),
        grid=(nt,),
        in_specs=[
            pl.BlockSpec((1, 1, TOP_K * tt), lambda i: (i, 0, 0), memory_space=pltpu.SMEM),
            pl.BlockSpec((1, 1, TOP_K * tt), lambda i: (jnp.minimum(i + 1, nt - 1), 0, 0), memory_space=pltpu.SMEM),
            pl.BlockSpec((tt, d), row),
            pl.BlockSpec((tt, LANES), row),
            pl.BlockSpec(memory_space=pl.ANY),
        ],
        out_specs=pl.BlockSpec((tt, d), row),
        out_shape=jax.ShapeDtypeStruct((n, d), jnp.float32),
        scratch_shapes=[pltpu.VMEM((2, TOP_K * tt, d), jnp.float32), pltpu.SemaphoreType.DMA((2,))],
        compiler_params=_cparams(("arbitrary",)),
        name="moe_combine",
    )(pos_t, pos_t, h, gates, ys)


def _moe_layout(top_i, n, tm):
    i32 = jnp.int32
    flat_e = top_i.reshape(-1)
    n_pairs = flat_e.shape[0]
    eids = jnp.arange(N_EXPERTS, dtype=i32)
    counts = jnp.sum(flat_e[:, None] == eids[None, :], axis=0, dtype=i32)
    nblk_e = (counts + tm - 1) // tm
    blk_end = jnp.cumsum(nblk_e)
    n_blk = (n_pairs + N_EXPERTS * (tm - 1)) // tm + 1
    n_rows = n_blk * tm
    blk = jnp.arange(n_blk, dtype=i32)
    blk_valid = (blk < blk_end[-1]).astype(i32)
    blk_expert = jnp.minimum(jnp.sum(blk[:, None] >= blk_end[None, :], axis=1, dtype=i32), N_EXPERTS - 1)
    last_e = jnp.max(jnp.where(nblk_e > 0, eids, 0))
    blk_expert = jnp.where(blk_valid > 0, blk_expert, last_e)
    fill_end = jnp.cumsum(nblk_e * tm - counts)
    filler = jnp.arange(n_rows - n_pairs, dtype=i32)
    fill_e = jnp.sum(filler[:, None] >= fill_end[None, :], axis=1, dtype=i32)
    key = jnp.concatenate([flat_e * 2, fill_e * 2 + 1])
    tok = jnp.concatenate([jnp.arange(n_pairs, dtype=i32) // TOP_K, jnp.zeros_like(filler)])
    _, tok_rows, src = lax.sort((key, tok, jnp.arange(n_rows, dtype=i32)), num_keys=1)
    _, row_of = lax.sort((src, jnp.arange(n_rows, dtype=i32)), num_keys=1)
    return blk_expert, blk_valid, tok_rows, row_of[:n_pairs].reshape(n, TOP_K)


def _pack_in_proj(w_in, d_model):
    hk = d_model // 2
    hv = d_model
    o = 0
    cuts = {}
    for name, width in (("gq", hk), ("gk", hk), ("gv", hv), ("gr", hv), ("ga", GLA_RANK), ("aq", d_model),
                        ("ak", d_model // GROUP), ("av", d_model // GROUP), ("iq", IDX_HEADS * IDX_DIM),
                        ("ik", IDX_DIM), ("iw", IDX_HEADS), ("za", d_model), ("zb", d_model)):
        cuts[name] = (o, o + width)
        o += width
    assert o == w_in.shape[1]
    col = lambda a, b: w_in[:, cuts[a][0]:cuts[b][1]]
    w1 = col("gq", "gr")
    w2 = col("aq", "av")
    w4 = col("za", "zb")
    k = w_in.shape[0]
    iq = jnp.pad(col("iq", "iq").reshape(k, IDX_HEADS, IDX_DIM), ((0, 0), (0, 0), (0, LANES - IDX_DIM)))
    pad_to = lambda a: jnp.pad(a, ((0, 0), (0, LANES - a.shape[1])))
    w3 = jnp.concatenate([iq.reshape(k, IDX_HEADS * LANES), pad_to(col("ik", "ik")), pad_to(col("ga", "ga")),
                          pad_to(col("iw", "iw"))], axis=1)
    bf = lambda a: a.astype(jnp.bfloat16)
    return bf(w1), bf(w2), bf(w3), bf(w4)


def kernel(x_prompt, x_sample, cache_k, cache_v, cache_kidx, state_gla, norm1_g, w_in, w_alpha2, b_alpha, gla_norm_g, q_norm_g, k_norm_g, w_proj_a, w_proj_b, w_out, norm2_g, w_router, b_router, w_gu, b_gu, w_down, b_down):
    B, T, D = x_prompt.shape
    DB, DS, _ = x_sample.shape
    depth = norm1_g.shape[0]
    NP, NS = B * T, DB * DS
    N = NP + NS
    past = cache_k.shape[2]
    hd = D // ATT_HEADS
    dk, dv = D // (2 * GLA_HEADS), D // GLA_HEADS
    kvw = ATT_KV_HEADS * hd
    bf16 = jnp.bfloat16

    pos = jnp.concatenate([jnp.tile(jnp.arange(T, dtype=jnp.int32), B),
                           jnp.tile(past + jnp.arange(DS, dtype=jnp.int32), DB)])
    x = jnp.concatenate([x_prompt.reshape(NP, D), x_sample.reshape(NS, D)], axis=0)
    outs = {k: [] for k in ("kp", "vp", "kip", "sp", "ks", "vs", "kis", "ss")}

    for l in range(depth):
        w1, w2, w3, w4 = _pack_in_proj(w_in[l], D)
        xn = _rmsnorm(x, norm1_g[l], bf16)
        z1 = _matmul(xn, w1, name="in_proj_gla")
        z2 = _matmul(xn, w2, name="in_proj_attn")
        z3 = _matmul(xn, w3, name="in_proj_idx")
        z4 = _matmul(xn, w4, name="in_proj_gate")

        w2p = jnp.pad(w_alpha2[l], ((0, LANES - GLA_RANK), (0, 0)))
        oa_p, S_p = _gla(z1, z3, 0, B, T, w2p, b_alpha[l], gla_norm_g[l], None, dk, dv)
        oa_s, S_s = _gla(z1, z3, NP, DB, DS, w2p, b_alpha[l], gla_norm_g[l], state_gla[l], dk, dv)
        o_a = jnp.concatenate([oa_p, oa_s], axis=0)

        q_hm, k_f, k_b, v_b, qc_hm, ki_f, kc, w_pad = _prep(z2, z3, q_norm_g[l], k_norm_g[l], pos, hd)
        v_f = z2[:, ATT_HEADS * hd + kvw:]
        topk_p = min(TOPK_MAX, T // 4)
        tk_p = 1024 if T % 1024 == 0 else (256 if T % 256 == 0 else T)
        ob_p = _sparse_attention(q_hm, qc_hm, w_pad, kc, k_b, v_b, B, T, T, T, 0, topk_p, 128, tk_p, 0)
        lk_s = past + DS
        lk_pad = -(-lk_s // LANES) * LANES
        padk = lambda a: jnp.pad(a, ((0, 0), (0, lk_pad - lk_s), (0, 0)))
        ck_hi = cache_kidx[l].astype(bf16)
        ck_lo = (cache_kidx[l] - ck_hi.astype(jnp.float32)).astype(bf16)
        kc_s = padk(jnp.concatenate([jnp.concatenate([ck_hi, ck_hi, ck_lo, ck_lo], axis=-1),
                                     kc[NP:].reshape(DB, DS, -1)], axis=1))
        kb_s = padk(jnp.concatenate([cache_k[l].reshape(DB, past, kvw).astype(bf16),
                                     k_b[NP:].reshape(DB, DS, kvw)], axis=1))
        vb_s = padk(jnp.concatenate([cache_v[l].reshape(DB, past, kvw).astype(bf16),
                                     v_b[NP:].reshape(DB, DS, kvw)], axis=1))
        topk_s = min(TOPK_MAX, lk_s // 4)
        flat = lambda a: a.reshape(DB * lk_pad, a.shape[-1])
        ob_s = _sparse_attention(q_hm, qc_hm, w_pad, flat(kc_s), flat(kb_s), flat(vb_s),
                                 DB, DS, lk_pad, lk_s, past, topk_s, DS, lk_pad, NP)
        o_b = jnp.concatenate([ob_p, ob_s], axis=0)

        mixed = _merge(o_a, o_b, w_proj_a[l].astype(bf16), w_proj_b[l].astype(bf16), z4)
        h = _resid_matmul(x, mixed, w_out[l].astype(bf16))

        xn2, top_i, gates = _router(h, norm2_g[l], w_router[l], b_router[l])
        tm_e = 512 if N >= 8192 else 64
        blk_e, blk_v, tok_rows, pos_rows = _moe_layout(top_i[:, :TOP_K], N, tm_e)
        ys = _experts(xn2, blk_e, blk_v, tok_rows, w_gu[l].astype(bf16), b_gu[l], w_down[l].astype(bf16),
                      b_down[l], tm_e)
        x = _combine(h, gates, pos_rows, ys)

        outs["kp"].append(k_f[:NP].reshape(B, T, ATT_KV_HEADS, hd))
        outs["vp"].append(v_f[:NP].reshape(B, T, ATT_KV_HEADS, hd))
        outs["kip"].append(ki_f[:NP].reshape(B, T, IDX_DIM))
        outs["sp"].append(S_p)
        outs["ks"].append(k_f[NP:].reshape(DB, DS, ATT_KV_HEADS, hd))
        outs["vs"].append(v_f[NP:].reshape(DB, DS, ATT_KV_HEADS, hd))
        outs["kis"].append(ki_f[NP:].reshape(DB, DS, IDX_DIM))
        outs["ss"].append(S_s)

    st = lambda k: jnp.stack(outs[k], axis=0)
    return (x[:NP].reshape(B, T, D), x[NP:].reshape(DB, DS, D), st("kp"), st("vp"), st("kip"), st("sp"),
            st("ks"), st("vs"), st("kis"), st("ss"))
```

```python
import functools
import math

import jax
import jax.numpy as jnp
from jax import lax
from jax.experimental import pallas as pl
from jax.experimental.pallas import tpu as pltpu

CHUNK = 64
GLA_HEADS = 4
GLA_RANK = 16
GLA_TAU = 16.0
ATT_HEADS = 16
ATT_KV_HEADS = 4
GROUP = ATT_HEADS // ATT_KV_HEADS
ROPE_THETA = 500000.0
IDX_HEADS = 8
IDX_DIM = 64
TOPK_MAX = 256
N_EXPERTS = 32
TOP_K = 4
SWIGLU_LIMIT = 7.0
SWIGLU_ALPHA = 1.702
EPS = 1e-6

LANES = 128
VMEM_LIMIT = 56 * 1024 * 1024
INT_MIN = -(2 ** 31)
ATTN_ROW_CHUNK = 16
SCORE_COLS = 256
LOG2E = 1.4426950408889634
NEG_INIT = -1e30
MASK_BIG = 2.0 ** 100
M_INIT = -(2.0 ** 120)

_HIGHEST = lax.Precision.HIGHEST
_NT = (((1,), (1,)), ((), ()))
_TN = (((0,), (0,)), ((), ()))


def _pick(n, cands):
    for c in cands:
        if n % c == 0:
            return c
    raise ValueError(f"no tile in {cands} divides {n}")


def _cparams(sem):
    return pltpu.CompilerParams(dimension_semantics=sem, vmem_limit_bytes=VMEM_LIMIT)


def _rmsnorm_kernel(x_ref, g_ref, o_ref):
    x = x_ref[...]
    y = x * lax.rsqrt(jnp.mean(x * x, axis=-1, keepdims=True) + EPS) * g_ref[...]
    o_ref[...] = y.astype(o_ref.dtype)


def _rmsnorm(x, g, out_dtype):
    n, d = x.shape
    tm = _pick(n, (768, 512, 256, 128, 64))
    return pl.pallas_call(
        _rmsnorm_kernel,
        grid=(n // tm,),
        in_specs=[pl.BlockSpec((tm, d), lambda i: (i, 0)), pl.BlockSpec((1, d), lambda i: (0, 0))],
        out_specs=pl.BlockSpec((tm, d), lambda i: (i, 0)),
        out_shape=jax.ShapeDtypeStruct((n, d), out_dtype),
        compiler_params=_cparams(("parallel",)),
        name="rmsnorm",
    )(x, g.reshape(1, d))


def _mm_kernel(a_ref, w_ref, o_ref):
    o_ref[...] = jnp.dot(a_ref[...], w_ref[...], preferred_element_type=jnp.float32).astype(o_ref.dtype)


def _matmul(a, w, out_dtype=jnp.float32, name="matmul"):
    m, k = a.shape
    _, n = w.shape
    tm = _pick(m, (768, 512, 256, 128, 64))
    tn = 1024 if n % 1024 == 0 else (512 if n % 512 == 0 else n)
    return pl.pallas_call(
        _mm_kernel,
        grid=(m // tm, n // tn),
        in_specs=[pl.BlockSpec((tm, k), lambda i, j: (i, 0)), pl.BlockSpec((k, tn), lambda i, j: (0, j))],
        out_specs=pl.BlockSpec((tm, tn), lambda i, j: (i, j)),
        out_shape=jax.ShapeDtypeStruct((m, n), out_dtype),
        compiler_params=_cparams(("parallel", "arbitrary")),
        name=name,
    )(a, w)


def _merge_kernel(oa_ref, ob_ref, wa_ref, wb_ref, za_ref, zb_ref, o_ref):
    pa = jnp.dot(oa_ref[...], wa_ref[...], preferred_element_type=jnp.float32)
    pb = jnp.dot(ob_ref[...], wb_ref[...], preferred_element_type=jnp.float32)
    o_ref[...] = (jax.nn.sigmoid(za_ref[...]) * pa + jax.nn.sigmoid(zb_ref[...]) * pb).astype(o_ref.dtype)


def _merge(o_a, o_b, wa, wb, z4):
    m, k = o_a.shape
    n = wa.shape[1]
    tm = _pick(m, (768, 512, 256, 128, 64))
    tn = 512
    nj = n // tn
    return pl.pallas_call(
        _merge_kernel,
        grid=(m // tm, nj),
        in_specs=[
            pl.BlockSpec((tm, k), lambda i, j: (i, 0)),
            pl.BlockSpec((tm, k), lambda i, j: (i, 0)),
            pl.BlockSpec((k, tn), lambda i, j: (0, j)),
            pl.BlockSpec((k, tn), lambda i, j: (0, j)),
            pl.BlockSpec((tm, tn), lambda i, j: (i, j)),
            pl.BlockSpec((tm, tn), lambda i, j: (i, j + nj)),
        ],
        out_specs=pl.BlockSpec((tm, tn), lambda i, j: (i, j)),
        out_shape=jax.ShapeDtypeStruct((m, n), jnp.bfloat16),
        compiler_params=_cparams(("parallel", "arbitrary")),
        name="merge",
    )(o_a, o_b, wa, wb, z4, z4)


def _resid_mm_kernel(x_ref, a_ref, w_ref, o_ref):
    o_ref[...] = x_ref[...] + jnp.dot(a_ref[...], w_ref[...], preferred_element_type=jnp.float32)


def _resid_matmul(x, a, w):
    m, k = a.shape
    n = w.shape[1]
    tm = _pick(m, (768, 512, 256, 128, 64))
    tn = 1024 if n % 1024 == 0 else n
    return pl.pallas_call(
        _resid_mm_kernel,
        grid=(m // tm, n // tn),
        in_specs=[
            pl.BlockSpec((tm, tn), lambda i, j: (i, j)),
            pl.BlockSpec((tm, k), lambda i, j: (i, 0)),
            pl.BlockSpec((k, tn), lambda i, j: (0, j)),
        ],
        out_specs=pl.BlockSpec((tm, tn), lambda i, j: (i, j)),
        out_shape=jax.ShapeDtypeStruct((m, n), jnp.float32),
        compiler_params=_cparams(("parallel", "arbitrary")),
        name="out_proj",
    )(x, a, w)


def _log_sigmoid(x):
    return jnp.minimum(x, 0.0) - jnp.log(1.0 + jnp.exp(-jnp.abs(x)))


def _gla_kernel(*refs, c, dk, dv, has_init):
    if has_init:
        (z1_ref, ga_ref, w2_ref, b_ref, gn_ref, s0_ref, o_ref, sout_ref, s_scr) = refs
    else:
        (z1_ref, ga_ref, w2_ref, b_ref, gn_ref, o_ref, sout_ref, s_scr) = refs
        s0_ref = None
    ci = pl.program_id(1)
    hk = GLA_HEADS * dk

    @pl.when(ci == 0)
    def _():
        if has_init:
            s_scr[...] = s0_ref[0]
        else:
            s_scr[...] = jnp.zeros_like(s_scr)

    ga = ga_ref[...]
    g = _log_sigmoid(jnp.dot(ga, w2_ref[...], precision=_HIGHEST,
                             preferred_element_type=jnp.float32) + b_ref[...]) / GLA_TAU
    row = lax.broadcasted_iota(jnp.int32, (c, c), 0)
    col = lax.broadcasted_iota(jnp.int32, (c, c), 1)
    causal = col <= row
    b = jnp.dot(causal.astype(jnp.float32), g, precision=_HIGHEST, preferred_element_type=jnp.float32)
    bend_col = jnp.broadcast_to(b[c - 1:c, :], (8, hk)).T[:, 0:1]
    scale = dk ** -0.5
    for h in range(GLA_HEADS):
        bh = b[:, h * dk:(h + 1) * dk]
        q = z1_ref[:, h * dk:(h + 1) * dk] * scale
        k = z1_ref[:, hk + h * dk: hk + (h + 1) * dk]
        v = z1_ref[:, 2 * hk + h * dv: 2 * hk + (h + 1) * dv].astype(jnp.bfloat16)
        r = z1_ref[:, 2 * hk + GLA_HEADS * dv + h * dv: 2 * hk + GLA_HEADS * dv + (h + 1) * dv]
        qe = (q * jnp.exp(bh)).astype(jnp.bfloat16)
        ke = (k * jnp.exp(-bh)).astype(jnp.bfloat16)
        a = lax.dot_general(qe, ke, _NT, preferred_element_type=jnp.float32)
        a = jnp.where(causal, a, 0.0).astype(jnp.bfloat16)
        s = s_scr[h]
        o = (jnp.dot(a, v, preferred_element_type=jnp.float32)
             + jnp.dot(qe, s.astype(jnp.bfloat16), preferred_element_type=jnp.float32))
        b_end = bh[c - 1:c, :]
        kd = (k * jnp.exp(b_end - bh)).astype(jnp.bfloat16)
        s_scr[h] = (jnp.exp(bend_col[h * dk:(h + 1) * dk, :]) * s
                    + lax.dot_general(kd, v, _TN, preferred_element_type=jnp.float32))
        on = o * lax.rsqrt(jnp.mean(o * o, axis=-1, keepdims=True) + EPS) * gn_ref[...]
        o_ref[:, h * dv:(h + 1) * dv] = (on * (r * jax.nn.sigmoid(r))).astype(o_ref.dtype)

    @pl.when(ci == pl.num_programs(1) - 1)
    def _():
        sout_ref[0] = s_scr[...]


def _gla(z1, z3, row0, n_seq, seq_len, w2p, b_alpha, gn, s0, dk, dv):
    c = CHUNK if seq_len % CHUNK == 0 else seq_len
    nc = seq_len // c
    hk = GLA_HEADS * dk
    blk0 = row0 // c
    ga_blk = (IDX_HEADS + 1)
    has_init = s0 is not None
    rows = n_seq * seq_len
    in_specs = [
        pl.BlockSpec((c, z1.shape[1]), lambda s, i: (blk0 + s * nc + i, 0)),
        pl.BlockSpec((c, LANES), lambda s, i: (blk0 + s * nc + i, ga_blk)),
        pl.BlockSpec((LANES, hk), lambda s, i: (0, 0)),
        pl.BlockSpec((1, hk), lambda s, i: (0, 0)),
        pl.BlockSpec((1, dv), lambda s, i: (0, 0)),
    ]
    args = [z1, z3, w2p, b_alpha.reshape(1, hk), gn.reshape(1, dv)]
    if has_init:
        in_specs.append(pl.BlockSpec((1, GLA_HEADS, dk, dv), lambda s, i: (s, 0, 0, 0)))
        args.append(s0)
    return pl.pallas_call(
        functools.partial(_gla_kernel, c=c, dk=dk, dv=dv, has_init=has_init),
        grid=(n_seq, nc),
        in_specs=in_specs,
        out_specs=[
            pl.BlockSpec((c, GLA_HEADS * dv), lambda s, i: (s * nc + i, 0)),
            pl.BlockSpec((1, GLA_HEADS, dk, dv), lambda s, i: (s, 0, 0, 0)),
        ],
        out_shape=[
            jax.ShapeDtypeStruct((rows, GLA_HEADS * dv), jnp.bfloat16),
            jax.ShapeDtypeStruct((n_seq, GLA_HEADS, dk, dv), jnp.float32),
        ],
        scratch_shapes=[pltpu.VMEM((GLA_HEADS, dk, dv), jnp.float32)],
        compiler_params=_cparams(("parallel", "arbitrary")),
        name="gla_init" if has_init else "gla",
    )(*args)


def _rope(x, cos, sin, half):
    lane = lax.broadcasted_iota(jnp.int32, x.shape, 1)
    partner = jnp.where(lane < half, pltpu.roll(x, LANES - half, 1), pltpu.roll(x, half, 1))
    return x * cos + partner * sin


def _split_hi_lo(x):
    hi = x.astype(jnp.bfloat16).astype(jnp.float32)
    lo = (x - hi).astype(jnp.bfloat16).astype(jnp.float32)
    return hi, lo


def _prep_kernel(z2_ref, z3_ref, qg_ref, kg_ref, ca_ref, sa_ref, ci_ref, si_ref,
                 q_ref, kf_ref, kb_ref, vb_ref, qc_ref, kif_ref, kc_ref, w_ref, *, hd):
    ca, sa, ci, si = ca_ref[...], sa_ref[...], ci_ref[...], si_ref[...]
    qscale = hd ** -0.5 * LOG2E

    def headnorm(x, g):
        return x * lax.rsqrt(jnp.mean(x * x, axis=-1, keepdims=True) + EPS) * g

    for h in range(ATT_HEADS):
        x = headnorm(z2_ref[:, h * hd:(h + 1) * hd], qg_ref[...])
        q_ref[h] = (_rope(x, ca, sa, hd // 8) * qscale).astype(q_ref.dtype)
    k0 = ATT_HEADS * hd
    for h in range(ATT_KV_HEADS):
        x = headnorm(z2_ref[:, k0 + h * hd:k0 + (h + 1) * hd], kg_ref[...])
        y = _rope(x, ca, sa, hd // 8)
        kf_ref[:, h * hd:(h + 1) * hd] = y
        kb_ref[:, h * hd:(h + 1) * hd] = y.astype(kb_ref.dtype)
    v0 = k0 + ATT_KV_HEADS * hd
    vb_ref[...] = z2_ref[:, v0:v0 + ATT_KV_HEADS * hd].astype(vb_ref.dtype)
    for h in range(IDX_HEADS):
        y = _rope(z3_ref[:, h * LANES:(h + 1) * LANES], ci, si, IDX_DIM // 8)
        hi, lo = _split_hi_lo(y)
        t = (hi + pltpu.roll(lo, IDX_DIM, 1)).astype(qc_ref.dtype)
        qc_ref[h] = jnp.concatenate([t, t], axis=1)
    y = _rope(z3_ref[:, IDX_HEADS * LANES:(IDX_HEADS + 1) * LANES], ci, si, IDX_DIM // 8)
    kif_ref[...] = y[:, :IDX_DIM]
    hi, lo = _split_hi_lo(y)
    kc_ref[...] = jnp.concatenate([hi + pltpu.roll(hi, IDX_DIM, 1), lo + pltpu.roll(lo, IDX_DIM, 1)],
                                  axis=1).astype(kc_ref.dtype)
    w_ref[...] = z3_ref[:, (IDX_HEADS + 2) * LANES:(IDX_HEADS + 3) * LANES] * ((IDX_HEADS * IDX_DIM) ** -0.5)


def _rope_tables(pos, rot):
    half = rot // 2
    inv = ROPE_THETA ** (-2.0 * jnp.arange(half, dtype=jnp.float32) / rot)
    ang = pos.astype(jnp.float32)[:, None] * inv[None, :]
    cos, sin = jnp.cos(ang), jnp.sin(ang)
    n = pos.shape[0]
    ones = jnp.ones((n, LANES - rot), jnp.float32)
    zeros = jnp.zeros((n, LANES - rot), jnp.float32)
    return (jnp.concatenate([cos, cos, ones], axis=1), jnp.concatenate([-sin, sin, zeros], axis=1))


def _prep(z2, z3, q_norm_g, k_norm_g, pos, hd):
    n = z2.shape[0]
    tr = _pick(n, (256, 128, 64))
    ca, sa = _rope_tables(pos, hd // 4)
    ci, si = _rope_tables(pos, IDX_DIM // 4)
    row = lambda i: (i, 0)
    tab = pl.BlockSpec((tr, LANES), row)
    kvw = ATT_KV_HEADS * hd
    return pl.pallas_call(
        functools.partial(_prep_kernel, hd=hd),
        grid=(n // tr,),
        in_specs=[
            pl.BlockSpec((tr, z2.shape[1]), row),
            pl.BlockSpec((tr, z3.shape[1]), row),
            pl.BlockSpec((1, hd), lambda i: (0, 0)),
            pl.BlockSpec((1, hd), lambda i: (0, 0)),
            tab, tab, tab, tab,
        ],
        out_specs=[
            pl.BlockSpec((ATT_HEADS, tr, hd), lambda i: (0, i, 0)),
            pl.BlockSpec((tr, kvw), row),
            pl.BlockSpec((tr, kvw), row),
            pl.BlockSpec((tr, kvw), row),
            pl.BlockSpec((IDX_HEADS, tr, 4 * IDX_DIM), lambda i: (0, i, 0)),
            pl.BlockSpec((tr, IDX_DIM), row),
            pl.BlockSpec((tr, 4 * IDX_DIM), row),
            pl.BlockSpec((tr, LANES), row),
        ],
        out_shape=[
            jax.ShapeDtypeStruct((ATT_HEADS, n, hd), jnp.bfloat16),
            jax.ShapeDtypeStruct((n, kvw), jnp.float32),
            jax.ShapeDtypeStruct((n, kvw), jnp.bfloat16),
            jax.ShapeDtypeStruct((n, kvw), jnp.bfloat16),
            jax.ShapeDtypeStruct((IDX_HEADS, n, 4 * IDX_DIM), jnp.bfloat16),
            jax.ShapeDtypeStruct((n, IDX_DIM), jnp.float32),
            jax.ShapeDtypeStruct((n, 4 * IDX_DIM), jnp.bfloat16),
            jax.ShapeDtypeStruct((n, LANES), jnp.float32),
        ],
        compiler_params=_cparams(("parallel",)),
        name="attn_prep",
    )(z2, z3, q_norm_g.reshape(1, hd), k_norm_g.reshape(1, hd), ca, sa, ci, si)


def _needed_tiles(i, tq, tk, lk_true, pos_off):
    last = i * tq + pos_off + tq - 1
    max_limit = jnp.minimum((last // CHUNK + 1) * CHUNK, lk_true)
    return (max_limit + tk - 1) // tk


def _attn_kernel(qi_ref, kt_ref, q_ref, qc_ref, w_ref, kc_ref, k_ref, v_ref, o_ref,
                 keys_ref, hi_ref, lo_ref, thr_ref, s_ref, p_ref, alpha_ref, m_ref, acc_ref,
                 *, tq, tk, lk_true, pos_off, topk, nbits, hd):
    step = pl.program_id(1)
    i = qi_ref[step]
    kt = kt_ref[step]
    n_need = _needed_tiles(i, tq, tk, lk_true, pos_off)
    n_sub = tk // LANES
    rows = GROUP * tq
    rc = min(ATTN_ROW_CHUNK, tq)

    def tile_keys(j):
        return jnp.concatenate([keys_ref[j * n_sub + c] for c in range(n_sub)], axis=1)

    @pl.when(kt == 0)
    def _select():
        rowi = lax.broadcasted_iota(jnp.int32, (tq, 1), 0)
        limit = jnp.minimum(((i * tq + pos_off + rowi) // CHUNK + 1) * CHUNK, lk_true)
        k_eff = jnp.minimum(topk, limit).astype(jnp.float32)
        w = w_ref[...]
        qc_all = qc_ref[...].reshape(IDX_HEADS * tq, 4 * IDX_DIM)
        cw = SCORE_COLS if tk % SCORE_COLS == 0 else tk

        def score_body(j, carry):
            for cs in range(tk // cw):
                cols = pl.ds(pl.multiple_of(j * tk + cs * cw, cw), cw)
                s_all = lax.dot_general(qc_all, kc_ref[cols, :], _NT, preferred_element_type=jnp.float32)
                acc = jnp.zeros((tq, cw), jnp.float32)
                for h in range(IDX_HEADS):
                    acc = acc + jnp.maximum(s_all[h * tq:(h + 1) * tq], 0.0) * w[:, h:h + 1]
                bits = pltpu.bitcast(acc, jnp.int32)
                sgn = bits >> 31
                key = (bits ^ (sgn & 0x7FFFFFFF)) - sgn
                col = j * tk + cs * cw + lax.broadcasted_iota(jnp.int32, (tq, cw), 1)
                key = jnp.where(col < limit, key, INT_MIN)
                hi = (key >> 16).astype(jnp.int16)
                for c in range(cw // LANES):
                    slab = j * n_sub + cs * (cw // LANES) + c
                    keys_ref[slab] = key[:, c * LANES:(c + 1) * LANES]
                    hi_ref[slab] = hi[:, c * LANES:(c + 1) * LANES]
            return carry

        lax.fori_loop(0, n_need, score_body, 0)

        def count(pred):
            def body(j, acc):
                for c in range(n_sub):
                    kk = keys_ref[j * n_sub + c]
                    col = j * tk + c * LANES + lax.broadcasted_iota(jnp.int32, (tq, LANES), 1)
                    acc = acc + jnp.where(pred(kk, col), 1.0, 0.0)
                return acc
            acc = lax.fori_loop(0, n_need, body, jnp.zeros((tq, LANES), jnp.float32))
            return jnp.sum(acc, axis=1, keepdims=True)

        def count16(ref, pred):
            one = jnp.ones((tq, LANES), jnp.int16)
            zero = jnp.zeros((tq, LANES), jnp.int16)

            def body(j, acc):
                for c in range(n_sub):
                    acc = acc + jnp.where(pred(ref[j * n_sub + c]), one, zero)
                return acc
            acc = lax.fori_loop(0, n_need, body, zero)
            return jnp.sum(acc.astype(jnp.float32), axis=1, keepdims=True)

        def as16(t):
            return jnp.broadcast_to(t, (tq, LANES)).astype(jnp.int16)

        def search16(ref, target):
            def cond(state):
                b, _, _, exact = state
                return jnp.logical_and(b < 16, jnp.logical_not(exact))

            def body(state):
                b, cand, c_cand, _ = state
                t = cand + lax.shift_left(jnp.int32(1), 15 - b)
                t16 = as16(t)
                c = count16(ref, lambda kk: kk >= t16)
                take = c >= target
                c_cand = jnp.where(take, c, c_cand)
                exact = jnp.max(jnp.abs(c_cand - target)) < 0.5
                return b + 1, jnp.where(take, t, cand), c_cand, exact

            init = (jnp.int32(0), jnp.full((tq, 1), -32768, jnp.int32), jnp.full((tq, 1), 1e9, jnp.float32),
                    jnp.bool_(False))
            _, cand, _, exact = lax.while_loop(cond, body, init)
            return cand, exact

        thr_hi, _ = search16(hi_ref, k_eff)
        thr_hi16 = as16(thr_hi)
        k_lo = k_eff - count16(hi_ref, lambda kk: kk > thr_hi16)

        def lo_body(j, carry):
            for c in range(n_sub):
                key = keys_ref[j * n_sub + c]
                lo = jnp.where((key >> 16) == thr_hi, (key & 0xFFFF) - 32768, -32768)
                lo_ref[j * n_sub + c] = lo.astype(jnp.int16)
            return carry

        lax.fori_loop(0, n_need, lo_body, 0)
        thr_lo, exact = search16(lo_ref, k_lo)
        thr = thr_hi * 65536 + (thr_lo + 32768)
        thr_ref[...] = thr

        @pl.when(jnp.logical_not(exact))
        def _ties():
            c_gt = count(lambda kk, col: kk > thr)
            need = k_eff - c_gt

            def idx_body(b, cut):
                t = cut + lax.shift_left(jnp.int32(1), nbits - 1 - b)
                c = count(lambda kk, col: jnp.where(kk == thr, col, t) < t)
                return jnp.where(c <= need, t, cut)

            cut = lax.fori_loop(0, nbits, idx_body, jnp.zeros((tq, 1), jnp.int32))

            def fix_body(j, carry):
                for c in range(n_sub):
                    kk = keys_ref[j * n_sub + c]
                    col = j * tk + c * LANES + lax.broadcasted_iota(jnp.int32, (tq, LANES), 1)
                    drop = jnp.where(kk == thr, col, -1) >= cut
                    keys_ref[j * n_sub + c] = jnp.where(drop, INT_MIN, kk)
                return carry

            lax.fori_loop(0, n_need, fix_body, 0)

        m_ref[...] = jnp.full(m_ref.shape, M_INIT, jnp.float32)
        acc_ref[...] = jnp.zeros(acc_ref.shape, jnp.float32)

    def _attend():
        sel = tile_keys(kt) >= thr_ref[...]
        mask = jnp.where(sel, 0.0, -MASK_BIG)
        if tq < LANES:
            mask = jnp.concatenate([mask, jnp.zeros((LANES - tq, tk), jnp.float32)], axis=0)
        mask_t = mask.T.astype(k_ref.dtype)
        rowi = lax.broadcasted_iota(jnp.int32, (tq, LANES), 0)
        lanei = lax.broadcasted_iota(jnp.int32, (tq, LANES), 1)
        eye = jnp.where(rowi == lanei, 1.0, 0.0).astype(q_ref.dtype)
        eye4 = jnp.concatenate([eye] * GROUP, axis=0)

        def qk(kvh):
            q4 = jnp.concatenate([q_ref[kvh * GROUP:(kvh + 1) * GROUP].reshape(rows, hd), eye4], axis=1)
            kb = jnp.concatenate([k_ref[:, kvh * hd:(kvh + 1) * hd], mask_t], axis=1)
            s_ref[kvh % 2] = lax.dot_general(q4, kb, _NT, preferred_element_type=jnp.float32)

        qk(0)
        for kvh in range(ATT_KV_HEADS):
            if kvh + 1 < ATT_KV_HEADS:
                qk(kvh + 1)
            sb = kvh % 2
            for c in range(rows // rc):
                r0 = c * rc
                t = s_ref[sb, r0:r0 + rc, :]
                m_prev = m_ref[kvh, r0:r0 + rc, :]
                m_new = jnp.maximum(m_prev, jnp.max(t, axis=1, keepdims=True))
                p_ref[sb, r0:r0 + rc, :] = jnp.exp2((t - m_new).astype(p_ref.dtype))
                alpha_ref[sb, r0:r0 + rc, :] = jnp.exp2(m_prev - m_new)
                m_ref[kvh, r0:r0 + rc, :] = m_new
            vb = jnp.concatenate([v_ref[:, kvh * hd:(kvh + 1) * hd], jnp.ones((tk, LANES), v_ref.dtype)], axis=1)
            acc_ref[kvh] = alpha_ref[sb] * acc_ref[kvh] + jnp.dot(p_ref[sb], vb, preferred_element_type=jnp.float32)

    _attend()

    @pl.when(kt == n_need - 1)
    def _finish():
        for kvh in range(ATT_KV_HEADS):
            o = acc_ref[kvh, :, 0:hd] / acc_ref[kvh, :, hd:hd + 1]
            for g in range(GROUP):
                h = kvh * GROUP + g
                o_ref[:, h * hd:(h + 1) * hd] = o[g * tq:(g + 1) * tq].astype(o_ref.dtype)


def _sparse_attention(q_hm, qcat_hm, w_pad, kcat, kb, vb, n_grp, tq_total, lk_pad, lk_true, pos_off, topk, tq, tk,
                      q_row0):
    hd = q_hm.shape[-1]
    qb0 = q_row0 // tq
    n_kt = lk_pad // tk
    nq = tq_total // tq
    nbits = max(1, math.ceil(math.log2(lk_pad + 1)))
    pairs = [(i, kt) for i in range(nq)
             for kt in range(-(-min(((i * tq + pos_off + tq - 1) // CHUNK + 1) * CHUNK, lk_true) // tk))]
    qi_of = jnp.asarray([p[0] for p in pairs], jnp.int32)
    kt_of = jnp.asarray([p[1] for p in pairs], jnp.int32)
    qblk = lambda g, s, qi, kt: (0, qb0 + g * nq + qi[s], 0)
    qrow = lambda g, s, qi, kt: (qb0 + g * nq + qi[s], 0)
    orow = lambda g, s, qi, kt: (g * nq + qi[s], 0)
    kv_map = lambda g, s, qi, kt: (g * n_kt + kt[s], 0)
    kern = functools.partial(_attn_kernel, tq=tq, tk=tk, lk_true=lk_true, pos_off=pos_off,
                             topk=topk, nbits=nbits, hd=hd)
    grid_spec = pltpu.PrefetchScalarGridSpec(
        num_scalar_prefetch=2,
        grid=(n_grp, len(pairs)),
        in_specs=[
            pl.BlockSpec((ATT_HEADS, tq, hd), qblk),
            pl.BlockSpec((IDX_HEADS, tq, 4 * IDX_DIM), qblk),
            pl.BlockSpec((tq, LANES), qrow),
            pl.BlockSpec((lk_pad, 4 * IDX_DIM), lambda g, s, qi, kt: (g, 0)),
            pl.BlockSpec((tk, ATT_KV_HEADS * hd), kv_map),
            pl.BlockSpec((tk, ATT_KV_HEADS * hd), kv_map),
        ],
        out_specs=pl.BlockSpec((tq, ATT_HEADS * hd), orow),
        scratch_shapes=[
            pltpu.VMEM((lk_pad // LANES, tq, LANES), jnp.int32),
            pltpu.VMEM((lk_pad // LANES, tq, LANES), jnp.int16),
            pltpu.VMEM((lk_pad // LANES, tq, LANES), jnp.int16),
            pltpu.VMEM((tq, 1), jnp.int32),
            pltpu.VMEM((2, GROUP * tq, tk), jnp.float32),
            pltpu.VMEM((2, GROUP * tq, tk), jnp.bfloat16),
            pltpu.VMEM((2, GROUP * tq, 1), jnp.float32),
            pltpu.VMEM((ATT_KV_HEADS, GROUP * tq, 1), jnp.float32),
            pltpu.VMEM((ATT_KV_HEADS, GROUP * tq, hd + LANES), jnp.float32),
        ],
    )
    return pl.pallas_call(
        kern,
        grid_spec=grid_spec,
        out_shape=jax.ShapeDtypeStruct((n_grp * tq_total, ATT_HEADS * hd), jnp.bfloat16),
        compiler_params=_cparams(("parallel", "arbitrary")),
        name="sparse_attn",
    )(qi_of, kt_of, q_hm, qcat_hm, w_pad, kcat, kb, vb)


def _router_kernel(h_ref, g_ref, wr_ref, br_ref, xn_ref, ti_ref, tg_ref):
    h = h_ref[...]
    xn = h * lax.rsqrt(jnp.mean(h * h, axis=-1, keepdims=True) + EPS) * g_ref[...]
    xn_ref[...] = xn
    logits = jnp.dot(xn, wr_ref[...], precision=_HIGHEST, preferred_element_type=jnp.float32) + br_ref[...]
    lane = lax.broadcasted_iota(jnp.int32, logits.shape, 1)
    cur = logits
    vals, idxs = [], []
    for _ in range(TOP_K):
        m = jnp.max(cur, axis=1, keepdims=True)
        idx = jnp.min(jnp.where(cur == m, lane, LANES), axis=1, keepdims=True)
        vals.append(m)
        idxs.append(idx)
        cur = jnp.where(lane == idx, -jnp.inf, cur)
    es = [jnp.exp(v - vals[0]) for v in vals]
    denom = es[0] + es[1] + es[2] + es[3]
    ti = jnp.zeros(logits.shape, jnp.int32)
    tg = jnp.zeros(logits.shape, jnp.float32)
    for k in range(TOP_K):
        ti = jnp.where(lane == k, idxs[k], ti)
        tg = jnp.where(lane == k, es[k] / denom, tg)
    ti_ref[...] = ti
    tg_ref[...] = tg


def _router(h, g, w_router, b_router):
    n, d = h.shape
    tm = _pick(n, (768, 512, 256, 128, 64))
    wr = jnp.pad(w_router, ((0, 0), (0, LANES - N_EXPERTS)))
    br = jnp.pad(b_router, (0, LANES - N_EXPERTS), constant_values=NEG_INIT).reshape(1, LANES)
    row = lambda i: (i, 0)
    return pl.pallas_call(
        _router_kernel,
        grid=(n // tm,),
        in_specs=[
            pl.BlockSpec((tm, d), row),
            pl.BlockSpec((1, d), lambda i: (0, 0)),
            pl.BlockSpec((d, LANES), lambda i: (0, 0)),
            pl.BlockSpec((1, LANES), lambda i: (0, 0)),
        ],
        out_specs=[pl.BlockSpec((tm, d), row), pl.BlockSpec((tm, LANES), row), pl.BlockSpec((tm, LANES), row)],
        out_shape=[
            jax.ShapeDtypeStruct((n, d), jnp.float32),
            jax.ShapeDtypeStruct((n, LANES), jnp.int32),
            jax.ShapeDtypeStruct((n, LANES), jnp.float32),
        ],
        compiler_params=_cparams(("parallel",)),
        name="router",
    )(h, g.reshape(1, d), wr, br)


def _start_row_gather(idx_ref, n_rows, src_hbm, dst_ref, sem):
    def start(r, c):
        pltpu.make_async_copy(src_hbm.at[pl.ds(idx_ref[0, 0, r], 1), :], dst_ref.at[pl.ds(r, 1), :], sem).start()
        return c

    lax.fori_loop(0, n_rows, start, 0, unroll=8)


def _wait_row_gather(n_rows, src_hbm, dst_ref, sem):
    pltpu.make_async_copy(src_hbm.at[pl.ds(0, n_rows), :], dst_ref, sem).wait()


def _expert_kernel(be_ref, bv_ref, tok_ref, tok_next_ref, x_hbm, wg_ref, wu_ref, bg_ref, bu_ref, wd_ref, bd_ref,
                   o_ref, xs_ref, xb_ref, sem, *, tm, n_f, n_blk):
    b = pl.program_id(0)
    f = pl.program_id(1)
    valid = bv_ref[b] > 0
    slot = b % 2

    @pl.when(f == 0)
    def _():
        @pl.when(jnp.logical_and(b == 0, valid))
        def _():
            _start_row_gather(tok_ref, tm, x_hbm, xs_ref.at[0], sem.at[0])

        nxt = jnp.minimum(b + 1, n_blk - 1)

        @pl.when(jnp.logical_and(b + 1 < n_blk, bv_ref[nxt] > 0))
        def _():
            _start_row_gather(tok_next_ref, tm, x_hbm, xs_ref.at[1 - slot], sem.at[1 - slot])

        @pl.when(valid)
        def _():
            _wait_row_gather(tm, x_hbm, xs_ref.at[slot], sem.at[slot])
            xb_ref[...] = xs_ref[slot].astype(xb_ref.dtype)

        o_ref[...] = jnp.zeros_like(o_ref)

    @pl.when(valid)
    def _():
        xb = xb_ref[...]
        gate = jnp.dot(xb, wg_ref[0], preferred_element_type=jnp.float32) + bg_ref[0]
        up = jnp.dot(xb, wu_ref[0], preferred_element_type=jnp.float32) + bu_ref[0]
        gate = jnp.minimum(gate, SWIGLU_LIMIT)
        up = jnp.clip(up, -SWIGLU_LIMIT, SWIGLU_LIMIT)
        hh = (up + 1.0) * gate * jax.nn.sigmoid(SWIGLU_ALPHA * gate)
        o_ref[...] += jnp.dot(hh.astype(xb.dtype), wd_ref[0], preferred_element_type=jnp.float32)

    @pl.when(jnp.logical_and(valid, f == n_f - 1))
    def _():
        o_ref[...] += bd_ref[0]


def _experts(xn, blk_expert, blk_valid, tok_rows, w_gu, b_gu, w_down, b_down, tm):
    n_blk = blk_expert.shape[0]
    e, d, two_f = w_gu.shape
    dff = two_f // 2
    tf = 1024 if dff % 1024 == 0 else dff
    n_f = dff // tf

    def fidx(f, bv, b):
        return jnp.where(bv[b] > 0, f, n_f - 1)

    grid_spec = pltpu.PrefetchScalarGridSpec(
        num_scalar_prefetch=2,
        grid=(n_blk, n_f),
        in_specs=[
            pl.BlockSpec((1, 1, tm), lambda b, f, be, bv: (b, 0, 0), memory_space=pltpu.SMEM),
            pl.BlockSpec((1, 1, tm), lambda b, f, be, bv: (jnp.minimum(b + 1, n_blk - 1), 0, 0),
                         memory_space=pltpu.SMEM),
            pl.BlockSpec(memory_space=pl.ANY),
            pl.BlockSpec((1, d, tf), lambda b, f, be, bv: (be[b], 0, fidx(f, bv, b))),
            pl.BlockSpec((1, d, tf), lambda b, f, be, bv: (be[b], 0, n_f + fidx(f, bv, b))),
            pl.BlockSpec((1, 1, tf), lambda b, f, be, bv: (be[b], 0, fidx(f, bv, b))),
            pl.BlockSpec((1, 1, tf), lambda b, f, be, bv: (be[b], 0, n_f + fidx(f, bv, b))),
            pl.BlockSpec((1, tf, d), lambda b, f, be, bv: (be[b], fidx(f, bv, b), 0)),
            pl.BlockSpec((1, 1, d), lambda b, f, be, bv: (be[b], 0, 0)),
        ],
        out_specs=pl.BlockSpec((tm, d), lambda b, f, be, bv: (b, 0)),
        scratch_shapes=[
            pltpu.VMEM((2, tm, d), jnp.float32),
            pltpu.VMEM((tm, d), jnp.bfloat16),
            pltpu.SemaphoreType.DMA((2,)),
        ],
    )
    tok3 = tok_rows.reshape(n_blk, 1, tm)
    return pl.pallas_call(
        functools.partial(_expert_kernel, tm=tm, n_f=n_f, n_blk=n_blk),
        grid_spec=grid_spec,
        out_shape=jax.ShapeDtypeStruct((n_blk * tm, d), jnp.float32),
        compiler_params=_cparams(("arbitrary", "arbitrary")),
        name="experts",
    )(blk_expert, blk_valid, tok3, tok3, xn, w_gu, w_gu,
      b_gu.reshape(e, 1, two_f), b_gu.reshape(e, 1, two_f), w_down, b_down.reshape(e, 1, d))


def _combine_kernel(pos_ref, pos_next_ref, h_ref, g_ref, ys_hbm, o_ref, buf_ref, sem, *, tt, nt):
    i = pl.program_id(0)
    slot = i % 2
    n_rows = TOP_K * tt

    @pl.when(i == 0)
    def _():
        _start_row_gather(pos_ref, n_rows, ys_hbm, buf_ref.at[0], sem.at[0])

    @pl.when(i + 1 < nt)
    def _():
        _start_row_gather(pos_next_ref, n_rows, ys_hbm, buf_ref.at[1 - slot], sem.at[1 - slot])

    _wait_row_gather(n_rows, ys_hbm, buf_ref.at[slot], sem.at[slot])
    g = g_ref[...]
    y = h_ref[...]
    for k in range(TOP_K):
        y = y + buf_ref[slot, k * tt:(k + 1) * tt, :] * g[:, k:k + 1]
    o_ref[...] = y


def _combine(h, gates, pos, ys):
    n, d = h.shape
    tt = _pick(n, (384, 256, 128, 64))
    nt = n // tt
    pos_t = pos.reshape(nt, tt, TOP_K).transpose(0, 2, 1).reshape(nt, 1, TOP_K * tt)
    row = lambda i: (i, 0)
    return pl.pallas_call(
        functools.partial(_combine_kernel, tt=tt, nt=nt),
        grid=(nt,),
        in_specs=[
            pl.BlockSpec((1, 1, TOP_K * tt), lambda i: (i, 0, 0), memory_space=pltpu.SMEM),
            pl.BlockSpec((1, 1, TOP_K * tt), lambda i: (jnp.minimum(i + 1, nt - 1), 0, 0), memory_space=pltpu.SMEM),
            pl.BlockSpec((tt, d), row),
            pl.BlockSpec((tt, LANES), row),
            pl.BlockSpec(memory_space=pl.ANY),
        ],
        out_specs=pl.BlockSpec((tt, d), row),
        out_shape=jax.ShapeDtypeStruct((n, d), jnp.float32),
        scratch_shapes=[pltpu.VMEM((2, TOP_K * tt, d), jnp.float32), pltpu.SemaphoreType.DMA((2,))],
        compiler_params=_cparams(("arbitrary",)),
        name="moe_combine",
    )(pos_t, pos_t, h, gates, ys)


def _moe_layout(top_i, n, tm):
    i32 = jnp.int32
    flat_e = top_i.reshape(-1)
    n_pairs = flat_e.shape[0]
    eids = jnp.arange(N_EXPERTS, dtype=i32)
    counts = jnp.sum(flat_e[:, None] == eids[None, :], axis=0, dtype=i32)
    nblk_e = (counts + tm - 1) // tm
    blk_end = jnp.cumsum(nblk_e)
    n_blk = (n_pairs + N_EXPERTS * (tm - 1)) // tm + 1
    n_rows = n_blk * tm
    blk = jnp.arange(n_blk, dtype=i32)
    blk_valid = (blk < blk_end[-1]).astype(i32)
    blk_expert = jnp.minimum(jnp.sum(blk[:, None] >= blk_end[None, :], axis=1, dtype=i32), N_EXPERTS - 1)
    last_e = jnp.max(jnp.where(nblk_e > 0, eids, 0))
    blk_expert = jnp.where(blk_valid > 0, blk_expert, last_e)
    fill_end = jnp.cumsum(nblk_e * tm - counts)
    filler = jnp.arange(n_rows - n_pairs, dtype=i32)
    fill_e = jnp.sum(filler[:, None] >= fill_end[None, :], axis=1, dtype=i32)
    key = jnp.concatenate([flat_e * 2, fill_e * 2 + 1])
    tok = jnp.concatenate([jnp.arange(n_pairs, dtype=i32) // TOP_K, jnp.zeros_like(filler)])
    _, tok_rows, src = lax.sort((key, tok, jnp.arange(n_rows, dtype=i32)), num_keys=1)
    _, row_of = lax.sort((src, jnp.arange(n_rows, dtype=i32)), num_keys=1)
    return blk_expert, blk_valid, tok_rows, row_of[:n_pairs].reshape(n, TOP_K)


def _pack_in_proj(w_in, d_model):
    hk = d_model // 2
    hv = d_model
    o = 0
    cuts = {}
    for name, width in (("gq", hk), ("gk", hk), ("gv", hv), ("gr", hv), ("ga", GLA_RANK), ("aq", d_model),
                        ("ak", d_model // GROUP), ("av", d_model // GROUP), ("iq", IDX_HEADS * IDX_DIM),
                        ("ik", IDX_DIM), ("iw", IDX_HEADS), ("za", d_model), ("zb", d_model)):
        cuts[name] = (o, o + width)
        o += width
    assert o == w_in.shape[1]
    col = lambda a, b: w_in[:, cuts[a][0]:cuts[b][1]]
    w1 = col("gq", "gr")
    w2 = col("aq", "av")
    w4 = col("za", "zb")
    k = w_in.shape[0]
    iq = jnp.pad(col("iq", "iq").reshape(k, IDX_HEADS, IDX_DIM), ((0, 0), (0, 0), (0, LANES - IDX_DIM)))
    pad_to = lambda a: jnp.pad(a, ((0, 0), (0, LANES - a.shape[1])))
    w3 = jnp.concatenate([iq.reshape(k, IDX_HEADS * LANES), pad_to(col("ik", "ik")), pad_to(col("ga", "ga")),
                          pad_to(col("iw", "iw"))], axis=1)
    bf = lambda a: a.astype(jnp.bfloat16)
    return bf(w1), bf(w2), bf(w3), bf(w4)


def kernel(x_prompt, x_sample, cache_k, cache_v, cache_kidx, state_gla, norm1_g, w_in, w_alpha2, b_alpha, gla_norm_g, q_norm_g, k_norm_g, w_proj_a, w_proj_b, w_out, norm2_g, w_router, b_router, w_gu, b_gu, w_down, b_down):
    B, T, D = x_prompt.shape
    DB, DS, _ = x_sample.shape
    depth = norm1_g.shape[0]
    NP, NS = B * T, DB * DS
    N = NP + NS
    past = cache_k.shape[2]
    hd = D // ATT_HEADS
    dk, dv = D // (2 * GLA_HEADS), D // GLA_HEADS
    kvw = ATT_KV_HEADS * hd
    bf16 = jnp.bfloat16

    pos = jnp.concatenate([jnp.tile(jnp.arange(T, dtype=jnp.int32), B),
                           jnp.tile(past + jnp.arange(DS, dtype=jnp.int32), DB)])
    x = jnp.concatenate([x_prompt.reshape(NP, D), x_sample.reshape(NS, D)], axis=0)
    outs = {k: [] for k in ("kp", "vp", "kip", "sp", "ks", "vs", "kis", "ss")}

    for l in range(depth):
        w1, w2, w3, w4 = _pack_in_proj(w_in[l], D)
        xn = _rmsnorm(x, norm1_g[l], bf16)
        z1 = _matmul(xn, w1, name="in_proj_gla")
        z2 = _matmul(xn, w2, name="in_proj_attn")
        z3 = _matmul(xn, w3, name="in_proj_idx")
        z4 = _matmul(xn, w4, name="in_proj_gate")

        w2p = jnp.pad(w_alpha2[l], ((0, LANES - GLA_RANK), (0, 0)))
        oa_p, S_p = _gla(z1, z3, 0, B, T, w2p, b_alpha[l], gla_norm_g[l], None, dk, dv)
        oa_s, S_s = _gla(z1, z3, NP, DB, DS, w2p, b_alpha[l], gla_norm_g[l], state_gla[l], dk, dv)
        o_a = jnp.concatenate([oa_p, oa_s], axis=0)

        q_hm, k_f, k_b, v_b, qc_hm, ki_f, kc, w_pad = _prep(z2, z3, q_norm_g[l], k_norm_g[l], pos, hd)
        v_f = z2[:, ATT_HEADS * hd + kvw:]
        topk_p = min(TOPK_MAX, T // 4)
        tk_p = 1024 if T % 1024 == 0 else (256 if T % 256 == 0 else T)
        ob_p = _sparse_attention(q_hm, qc_hm, w_pad, kc, k_b, v_b, B, T, T, T, 0, topk_p, 128, tk_p, 0)
        lk_s = past + DS
        lk_pad = -(-lk_s // LANES) * LANES
        padk = lambda a: jnp.pad(a, ((0, 0), (0, lk_pad - lk_s), (0, 0)))
        ck_hi = cache_kidx[l].astype(bf16)
        ck_lo = (cache_kidx[l] - ck_hi.astype(jnp.float32)).astype(bf16)
        kc_s = padk(jnp.concatenate([jnp.concatenate([ck_hi, ck_hi, ck_lo, ck_lo], axis=-1),
                                     kc[NP:].reshape(DB, DS, -1)], axis=1))
        kb_s = padk(jnp.concatenate([cache_k[l].reshape(DB, past, kvw).astype(bf16),
                                     k_b[NP:].reshape(DB, DS, kvw)], axis=1))
        vb_s = padk(jnp.concatenate([cache_v[l].reshape(DB, past, kvw).astype(bf16),
                                     v_b[NP:].reshape(DB, DS, kvw)], axis=1))
        topk_s = min(TOPK_MAX, lk_s // 4)
        flat = lambda a: a.reshape(DB * lk_pad, a.shape[-1])
        ob_s = _sparse_attention(q_hm, qc_hm, w_pad, flat(kc_s), flat(kb_s), flat(vb_s),
                                 DB, DS, lk_pad, lk_s, past, topk_s, DS, lk_pad, NP)
        o_b = jnp.concatenate([ob_p, ob_s], axis=0)

        mixed = _merge(o_a, o_b, w_proj_a[l].astype(bf16), w_proj_b[l].astype(bf16), z4)
        h = _resid_matmul(x, mixed, w_out[l].astype(bf16))

        xn2, top_i, gates = _router(h, norm2_g[l], w_router[l], b_router[l])
        tm_e = 512 if N >= 8192 else 64
        blk_e, blk_v, tok_rows, pos_rows = _moe_layout(top_i[:, :TOP_K], N, tm_e)
        ys = _experts(xn2, blk_e, blk_v, tok_rows, w_gu[l].astype(bf16), b_gu[l], w_down[l].astype(bf16),
                      b_down[l], tm_e)
        x = _combine(h, gates, pos_rows, ys)

        outs["kp"].append(k_f[:NP].reshape(B, T, ATT_KV_HEADS, hd))
        outs["vp"].append(v_f[:NP].reshape(B, T, ATT_KV_HEADS, hd))
        outs["kip"].append(ki_f[:NP].reshape(B, T, IDX_DIM))
        outs["sp"].append(S_p)
        outs["ks"].append(k_f[NP:].reshape(DB, DS, ATT_KV_HEADS, hd))
        outs["vs"].append(v_f[NP:].reshape(DB, DS, ATT_KV_HEADS, hd))
        outs["kis"].append(ki_f[NP:].reshape(DB, DS, IDX_DIM))
        outs["ss"].append(S_s)

    st = lambda k: jnp.stack(outs[k], axis=0)
    return (x[:NP].reshape(B, T, D), x[NP:].reshape(DB, DS, D), st("kp"), st("vp"), st("kip"), st("sp"),
            st("ks"), st("vs"), st("kis"), st("ss"))
```
